```python
import jax, jax.numpy as jnp
from jax import lax
import numpy as np

D_MODEL = 4096
BATCH = 1
SEQ = 16384
DEPTH = 2
DEC_BATCH = 32
DEC_SEQ = 32
PAST_LEN = 1024

CHUNK = 64
MIX_W = D_MODEL
FOX_HD = 128
FOX_W = MIX_W // 2
FOX_HEADS = FOX_W // FOX_HD
FOX_QBLOCK = 128
GLA_W = MIX_W // 4
GLA_HEADS = 4
GLA_DV = GLA_W // GLA_HEADS
GLA_DK = GLA_DV // 2
GLA_KW = GLA_HEADS * GLA_DK
GLA_RANK = 16
GLA_TAU = 16.0
POOL_W = MIX_W - FOX_W - GLA_W
POOL_GROUPS = 4
POOL_GC = POOL_W // POOL_GROUPS
POOL_WINDOWS = (2, 4, 8, 16)
POOL_HIST = 15
D_FF = 4 * D_MODEL
RMS_EPS = 1e-6
IN_SIZES = (FOX_W, FOX_W, FOX_W, FOX_HEADS, GLA_KW, GLA_KW, GLA_W, GLA_RANK, GLA_W, POOL_W)
IN_W = 3 * FOX_W + FOX_HEADS + 2 * GLA_KW + GLA_W + GLA_RANK + GLA_W + POOL_W

kernel_name = "hybrid_fox_gla_pool_stream_step"


def rmsnorm(x, g):
    xf = x.astype(jnp.float32)
    y = xf * lax.rsqrt(jnp.mean(xf * xf, axis=-1, keepdims=True) + RMS_EPS)
    return (y * g.astype(jnp.float32)).astype(x.dtype)


def split_cols(z):
    out = []
    off = 0
    for n in IN_SIZES:
        out.append(z[..., off:off + n])
        off += n
    return out


def fox_attend(q, k, v, cq, ck, q_pos, k_pos):
    s = jnp.einsum('bqhd,bkhd->bhqk', q, k, preferred_element_type=jnp.float32) * (FOX_HD ** -0.5)
    bias = jnp.swapaxes(cq, 1, 2)[:, :, :, None] - jnp.swapaxes(ck, 1, 2)[:, :, None, :]
    mask = k_pos[None, :] <= q_pos[:, None]
    p = jax.nn.softmax(jnp.where(mask, s + bias, -jnp.inf), axis=-1)
    return jnp.einsum('bhqk,bkhd->bqhd', p.astype(v.dtype), v)


def fox_prompt(q, k, v, logf):
    B, T, H, Dh = q.shape
    C = jnp.cumsum(logf, axis=1)
    nb = T // FOX_QBLOCK
    pos = jnp.arange(T)
    qb = q.reshape(B, nb, FOX_QBLOCK, H, Dh).swapaxes(0, 1)
    cb = C.reshape(B, nb, FOX_QBLOCK, H).swapaxes(0, 1)
    pb = pos.reshape(nb, FOX_QBLOCK)
    ob = lax.map(lambda a: fox_attend(a[0], k, v, a[1], C, a[2], pos), (qb, cb, pb))
    return ob.swapaxes(0, 1).reshape(B, T, H, Dh)


def fox_sample(q, k, v, logf, k_past, v_past, logf_past):
    P = k_past.shape[1]
    T = q.shape[1]
    ka = jnp.concatenate([k_past, k], axis=1)
    va = jnp.concatenate([v_past, v], axis=1)
    C = jnp.cumsum(jnp.concatenate([logf_past.astype(jnp.float32), logf], axis=1), axis=1)
    return fox_attend(q, ka, va, C[:, P:], C, P + jnp.arange(T), jnp.arange(P + T))


def gla_chunk(S, q, k, v, la):
    T = q.shape[1]
    qf, kf, vf = q.astype(jnp.float32), k.astype(jnp.float32), v.astype(jnp.float32)
    b = jnp.cumsum(la, axis=1)
    o_inter = jnp.einsum('bthk,bhkv->bthv', qf * jnp.exp(b), S)
    causal = jnp.tril(jnp.ones((T, T), dtype=bool))[None, :, :, None, None]
    diff = b[:, :, None] - b[:, None, :]
    dec = jnp.exp(jnp.where(causal, diff, -jnp.inf))
    A = jnp.einsum('bthk,bshk,btshk->bhts', qf, kf, dec)
    o_intra = jnp.einsum('bhts,bshv->bthv', A, vf)
    bl = b[:, -1]
    kdec = kf * jnp.exp(bl[:, None] - b)
    S_new = jnp.exp(bl)[..., None] * S + jnp.einsum('bshk,bshv->bhkv', kdec, vf)
    return S_new, o_inter + o_intra


def gla_prompt(q, k, v, la):
    B, T, H, dk = q.shape
    dv = v.shape[-1]
    nc = T // CHUNK

    def to_chunks(x):
        return x.reshape((B, nc, CHUNK) + x.shape[2:]).swapaxes(0, 1)

    S0 = jnp.zeros((B, H, dk, dv), jnp.float32)
    S, o = lax.scan(lambda S, xs: gla_chunk(S, *xs), S0,
                    (to_chunks(q), to_chunks(k), to_chunks(v), to_chunks(la)))
    return S, o.swapaxes(0, 1).reshape(B, T, H, dv)


def pool_mix(p, hist, pos0, w_pool, s_pool):
    B, T, _ = p.shape
    xa = jnp.concatenate([hist.astype(p.dtype), p], axis=1).astype(jnp.float32)
    cs = jnp.pad(jnp.cumsum(xa, axis=1), ((0, 0), (1, 0), (0, 0)))
    pos = pos0 + jnp.arange(T)
    outs = []
    for g, w in enumerate(POOL_WINDOWS):
        lo, hi = g * POOL_GC, (g + 1) * POOL_GC
        tot = cs[:, POOL_HIST + 1:POOL_HIST + 1 + T, lo:hi] - cs[:, POOL_HIST + 1 - w:POOL_HIST + 1 - w + T, lo:hi]
        cnt = jnp.minimum(pos + 1, w).astype(jnp.float32)[None, :, None]
        outs.append(tot / cnt)
    d = (jnp.concatenate(outs, axis=-1) - xa[:, POOL_HIST:]).astype(p.dtype)
    y = jnp.einsum('btgc,gcd->btgd', d.reshape(B, T, POOL_GROUPS, POOL_GC), w_pool).reshape(B, T, POOL_W)
    return y * s_pool, xa[:, -POOL_HIST:].astype(p.dtype)


def layer(h, lw, fox_past, gla_S, pool_hist, pos0):
    B, T, _ = h.shape
    a = rmsnorm(h, lw['g_pre_mix'])
    fq, fk, fv, ff, gq, gk, gv, ga, gr, pz = split_cols(a @ lw['w_in'])
    fq = fq.reshape(B, T, FOX_HEADS, FOX_HD)
    fk = fk.reshape(B, T, FOX_HEADS, FOX_HD)
    fv = fv.reshape(B, T, FOX_HEADS, FOX_HD)
    logf = jax.nn.log_sigmoid(ff.astype(jnp.float32) + lw['b_fox_f'].astype(jnp.float32))
    if fox_past is None:
        fo = fox_prompt(fq, fk, fv, logf)
    else:
        fo = fox_sample(fq, fk, fv, logf, *fox_past)
    gq = gq.reshape(B, T, GLA_HEADS, GLA_DK) * (GLA_DK ** -0.5)
    gk = gk.reshape(B, T, GLA_HEADS, GLA_DK)
    gv = gv.reshape(B, T, GLA_HEADS, GLA_DV)
    zg = ga @ lw['w_gla_a2'] + lw['b_gla_a']
    la = (jax.nn.log_sigmoid(zg.astype(jnp.float32)) / GLA_TAU).reshape(B, T, GLA_HEADS, GLA_DK)
    if gla_S is None:
        S_new, go = gla_prompt(gq, gk, gv, la)
    else:
        S_new, go = gla_chunk(gla_S.astype(jnp.float32), gq, gk, gv, la)
    go = rmsnorm(go.astype(h.dtype), lw['g_gla_norm']) * jax.nn.silu(gr.reshape(B, T, GLA_HEADS, GLA_DV))
    if pool_hist is None:
        pool_hist = jnp.zeros((B, POOL_HIST, POOL_W), h.dtype)
    po, new_hist = pool_mix(pz, pool_hist, pos0, lw['w_pool'], lw['s_pool'])
    mix = jnp.concatenate([fo.reshape(B, T, FOX_W), go.reshape(B, T, GLA_W), po], axis=-1)
    h = h + rmsnorm(mix @ lw['w_out'], lw['g_post_mix'])
    c = rmsnorm(h, lw['g_pre_mlp'])
    f = jnp.square(jax.nn.relu(c @ lw['w_up'])) @ lw['w_down']
    h = h + rmsnorm(f, lw['g_post_mlp'])
    return h, (fk, fv, logf.astype(h.dtype), S_new.astype(h.dtype), new_hist)


def setup_inputs(seed: int = 0) -> dict:
    key = jax.random.key(seed)
    ks = jax.random.split(key, 24)
    L = DEPTH

    def nrm(k, shape, scale):
        return jax.random.normal(k, shape, jnp.float32) * scale

    return {
        "x_prompt": nrm(ks[0], (BATCH, SEQ, D_MODEL), 1.0),
        "x_sample": nrm(ks[1], (DEC_BATCH, DEC_SEQ, D_MODEL), 1.0),
        "cache_fox_k": nrm(ks[2], (L, DEC_BATCH, PAST_LEN, FOX_HEADS, FOX_HD), 1.0),
        "cache_fox_v": nrm(ks[3], (L, DEC_BATCH, PAST_LEN, FOX_HEADS, FOX_HD), 1.0),
        "cache_fox_logf": jax.nn.log_sigmoid(3.0 + nrm(ks[4], (L, DEC_BATCH, PAST_LEN, FOX_HEADS), 1.0)),
        "state_gla": nrm(ks[5], (L, DEC_BATCH, GLA_HEADS, GLA_DK, GLA_DV), 1.0),
        "state_pool": nrm(ks[6], (L, DEC_BATCH, POOL_HIST, POOL_W), 1.0),
        "g_pre_mix": 1.0 + nrm(ks[7], (L, D_MODEL), 0.02),
        "w_in": nrm(ks[8], (L, D_MODEL, IN_W), D_MODEL ** -0.5),
        "b_fox_f": jnp.linspace(1.0, 6.0, FOX_HEADS)[None, :] + nrm(ks[9], (L, FOX_HEADS), 0.1),
        "w_gla_a2": nrm(ks[10], (L, GLA_RANK, GLA_KW), GLA_RANK ** -0.5),
        "b_gla_a": nrm(ks[11], (L, GLA_KW), 0.1),
        "g_gla_norm": 1.0 + nrm(ks[12], (L, GLA_DV), 0.02),
        "w_pool": nrm(ks[13], (L, POOL_GROUPS, POOL_GC, POOL_GC), POOL_GC ** -0.5),
        "s_pool": 1.0 + nrm(ks[14], (L, POOL_W), 0.02),
        "w_out": nrm(ks[15], (L, MIX_W, D_MODEL), MIX_W ** -0.5),
        "g_post_mix": 1.0 + nrm(ks[16], (L, D_MODEL), 0.02),
        "g_pre_mlp": 1.0 + nrm(ks[17], (L, D_MODEL), 0.02),
        "w_up": nrm(ks[18], (L, D_MODEL, D_FF), D_MODEL ** -0.5),
        "w_down": nrm(ks[19], (L, D_FF, D_MODEL), D_FF ** -0.5),
        "g_post_mlp": 1.0 + nrm(ks[20], (L, D_MODEL), 0.02),
    }


def reference(x_prompt, x_sample, cache_fox_k, cache_fox_v, cache_fox_logf, state_gla, state_pool,
              g_pre_mix, w_in, b_fox_f, w_gla_a2, b_gla_a, g_gla_norm, w_pool, s_pool, w_out,
              g_post_mix, g_pre_mlp, w_up, w_down, g_post_mlp):
    past = cache_fox_k.shape[2]
    hp, hs = x_prompt, x_sample
    pst = [[] for _ in range(5)]
    sst = [[] for _ in range(5)]
    for l in range(DEPTH):
        lw = dict(g_pre_mix=g_pre_mix[l], w_in=w_in[l], b_fox_f=b_fox_f[l], w_gla_a2=w_gla_a2[l],
                  b_gla_a=b_gla_a[l], g_gla_norm=g_gla_norm[l], w_pool=w_pool[l], s_pool=s_pool[l],
                  w_out=w_out[l], g_post_mix=g_post_mix[l], g_pre_mlp=g_pre_mlp[l], w_up=w_up[l],
                  w_down=w_down[l], g_post_mlp=g_post_mlp[l])
        hp, sp = layer(hp, lw, None, None, None, 0)
        hs, ss = layer(hs, lw, (cache_fox_k[l], cache_fox_v[l], cache_fox_logf[l]),
                       state_gla[l], state_pool[l], past)
        for i in range(5):
            pst[i].append(sp[i])
            sst[i].append(ss[i])
    return (hp, hs,
            jnp.stack(pst[0]), jnp.stack(pst[1]), jnp.stack(pst[2]), jnp.stack(pst[3]), jnp.stack(pst[4]),
            jnp.stack(sst[0]), jnp.stack(sst[1]), jnp.stack(sst[2]), jnp.stack(sst[3]), jnp.stack(sst[4]))
```

```python
import functools

import jax
import jax.numpy as jnp
from jax import lax
from jax.experimental import pallas as pl
from jax.experimental.pallas import tpu as pltpu

F32 = jnp.float32
BF16 = jnp.bfloat16

RMS_EPS = 1e-6
FOX_HD = 128
GLA_HEADS = 4
GLA_TAU = 16.0
GLA_CHUNK = 64
POOL_WINDOWS = (2, 4, 8, 16)
POOL_HIST = 15
POOL_CARRY = 16
LANES = 128
SUBLANES = 8
VMEM_LIMIT = 56 * 1024 * 1024


def _params(*sem):
    return pltpu.CompilerParams(dimension_semantics=sem, vmem_limit_bytes=VMEM_LIMIT)


def _pick(n, pref, mult=8):
    b = min(pref, n)
    b -= b % mult
    while b > 0 and n % b:
        b -= mult
    assert b > 0, (n, pref, mult)
    return b


def _rms(x):
    return x * lax.rsqrt(jnp.mean(x * x, axis=-1, keepdims=True) + RMS_EPS)


def _log_sigmoid(x):
    return jnp.minimum(x, 0.0) - jnp.log1p(jnp.exp(-jnp.abs(x)))


def _norm_first_body(h_ref, g_ref, a_ref):
    a_ref[...] = (_rms(h_ref[...]) * g_ref[...]).astype(a_ref.dtype)


def _norm_mid_body(h_ref, y_ref, gy_ref, g_ref, h_out, a_out):
    h = h_ref[...] + _rms(y_ref[...]) * gy_ref[...]
    h_out[...] = h
    a_out[...] = (_rms(h) * g_ref[...]).astype(a_out.dtype)


def _norm_last_body(h_ref, y_ref, gy_ref, h_out):
    h_out[...] = h_ref[...] + _rms(y_ref[...]) * gy_ref[...]


def _norm_first(h, g):
    M, D = h.shape
    bm = _pick(M, 256)
    row = pl.BlockSpec((bm, D), lambda i: (i, 0))
    vec = pl.BlockSpec((1, D), lambda i: (0, 0))
    return pl.pallas_call(
        _norm_first_body, grid=(M // bm,), in_specs=[row, vec], out_specs=row,
        out_shape=jax.ShapeDtypeStruct((M, D), BF16),
        compiler_params=_params("parallel"), name="norm_first")(h, g.reshape(1, D))


def _norm_mid(h, y, gy, g):
    M, D = h.shape
    bm = _pick(M, 256)
    row = pl.BlockSpec((bm, D), lambda i: (i, 0))
    vec = pl.BlockSpec((1, D), lambda i: (0, 0))
    return pl.pallas_call(
        _norm_mid_body, grid=(M // bm,), in_specs=[row, row, vec, vec], out_specs=[row, row],
        out_shape=[jax.ShapeDtypeStruct((M, D), F32), jax.ShapeDtypeStruct((M, D), BF16)],
        compiler_params=_params("parallel"), name="norm_mid")(h, y, gy.reshape(1, D), g.reshape(1, D))


def _norm_last(h, y, gy):
    M, D = h.shape
    bm = _pick(M, 256)
    row = pl.BlockSpec((bm, D), lambda i: (i, 0))
    vec = pl.BlockSpec((1, D), lambda i: (0, 0))
    return pl.pallas_call(
        _norm_last_body, grid=(M // bm,), in_specs=[row, row, vec], out_specs=row,
        out_shape=jax.ShapeDtypeStruct((M, D), F32),
        compiler_params=_params("parallel"), name="norm_last")(h, y, gy.reshape(1, D))


def _mm_body(x_ref, w_ref, *o_refs, relu2):
    acc = jnp.dot(x_ref[...], w_ref[...], preferred_element_type=F32)
    if relu2:
        acc = jnp.square(jnp.maximum(acc, 0.0))
    for o in o_refs:
        o[...] = acc.astype(o.dtype)


def _matmul(x, w, out_dtypes, *, name, relu2=False, bm=1024, bn=1024):
    M, K = x.shape
    N = w.shape[1]
    bm = _pick(M, bm)
    bn = _pick(N, bn, LANES)
    outs = pl.pallas_call(
        functools.partial(_mm_body, relu2=relu2), grid=(M // bm, N // bn),
        in_specs=[pl.BlockSpec((bm, K), lambda i, j: (i, 0)),
                  pl.BlockSpec((K, bn), lambda i, j: (0, j))],
        out_specs=[pl.BlockSpec((bm, bn), lambda i, j: (i, j)) for _ in out_dtypes],
        out_shape=[jax.ShapeDtypeStruct((M, N), dt) for dt in out_dtypes],
        compiler_params=_params("parallel", "arbitrary"), name=name)(x, w)
    return outs


def _mm_cat_body(*refs, splits):
    x_refs, w_ref, o_ref = refs[:-2], refs[-2], refs[-1]
    acc = None
    off = 0
    for x_ref, kk in zip(x_refs, splits):
        part = jnp.dot(x_ref[...], w_ref[off:off + kk, :], preferred_element_type=F32)
        acc = part if acc is None else acc + part
        off += kk
    o_ref[...] = acc


def _matmul_cat(xs, w, *, name, bm=1024, bn=1024):
    M = xs[0].shape[0]
    K, N = w.shape
    splits = tuple(x.shape[1] for x in xs)
    assert sum(splits) == K
    bm = _pick(M, bm)
    bn = _pick(N, bn, LANES)
    in_specs = [pl.BlockSpec((bm, kk), lambda i, j: (i, 0)) for kk in splits]
    in_specs.append(pl.BlockSpec((K, bn), lambda i, j: (0, j)))
    return pl.pallas_call(
        functools.partial(_mm_cat_body, splits=splits), grid=(M // bm, N // bn),
        in_specs=in_specs, out_specs=pl.BlockSpec((bm, bn), lambda i, j: (i, j)),
        out_shape=jax.ShapeDtypeStruct((M, N), F32),
        compiler_params=_params("parallel", "arbitrary"), name=name)(*xs, w)


def _mm_kacc_body(x_ref, w_ref, o_ref):
    k = pl.program_id(1)
    acc = jnp.dot(x_ref[...], w_ref[...], preferred_element_type=F32)

    @pl.when(k == 0)
    def _():
        o_ref[...] = acc

    @pl.when(k > 0)
    def _():
        o_ref[...] += acc


def _matmul_kacc(x, w, *, name, bm=512, bk=1024):
    M, K = x.shape
    N = w.shape[1]
    bm = _pick(M, bm)
    bk = _pick(K, bk, LANES)
    return pl.pallas_call(
        _mm_kacc_body, grid=(M // bm, K // bk),
        in_specs=[pl.BlockSpec((bm, bk), lambda i, k: (i, k)),
                  pl.BlockSpec((bk, N), lambda i, k: (k, 0))],
        out_specs=pl.BlockSpec((bm, N), lambda i, k: (i, 0)),
        out_shape=jax.ShapeDtypeStruct((M, N), F32),
        compiler_params=_params("parallel", "arbitrary"), name=name)(x, w)


def _gates_body(t_ref, bf_ref, w2_ref, bg_ref, logf_ref, la_ref):
    t = t_ref[...]
    logf_ref[...] = _log_sigmoid(t + bf_ref[...])
    zg = jnp.dot(t, w2_ref[...], preferred_element_type=F32) + bg_ref[...]
    la_ref[...] = _log_sigmoid(zg) * (1.0 / GLA_TAU)


def _gates(tail, bfox_pad, w2_pad, bgla):
    M = tail.shape[0]
    KW = w2_pad.shape[1]
    bm = _pick(M, 512)
    return pl.pallas_call(
        _gates_body, grid=(M // bm,),
        in_specs=[pl.BlockSpec((bm, LANES), lambda i: (i, 0)),
                  pl.BlockSpec((1, LANES), lambda i: (0, 0)),
                  pl.BlockSpec((LANES, KW), lambda i: (0, 0)),
                  pl.BlockSpec((1, KW), lambda i: (0, 0))],
        out_specs=[pl.BlockSpec((bm, LANES), lambda i: (i, 0)),
                   pl.BlockSpec((bm, KW), lambda i: (i, 0))],
        out_shape=[jax.ShapeDtypeStruct((M, LANES), F32), jax.ShapeDtypeStruct((M, KW), F32)],
        compiler_params=_params("parallel"), name="gates")(tail, bfox_pad, w2_pad, bgla)


def _row_cumsum(x):
    n = x.shape[0]
    row = lax.broadcasted_iota(jnp.int32, x.shape, 0)
    sh = 1
    while sh < n:
        x = x + jnp.where(row >= sh, pltpu.roll(x, sh, axis=0), 0.0)
        sh *= 2
    return x


def _cumsum_body(x_ref, c_ref, carry_sc):
    @pl.when(pl.program_id(1) == 0)
    def _():
        carry_sc[...] = jnp.zeros(carry_sc.shape, F32)

    c = _row_cumsum(x_ref[...]) + carry_sc[...]
    c_ref[...] = c
    n = c.shape[0]
    carry_sc[...] = c[n - 1:n, :]


def _cumsum(x, bl):
    S, L, W = x.shape
    bl = _pick(L, bl)
    spec = pl.BlockSpec((None, bl, W), lambda s, i: (s, i, 0))
    return pl.pallas_call(
        _cumsum_body, grid=(S, L // bl), in_specs=[spec], out_specs=spec,
        out_shape=jax.ShapeDtypeStruct((S, L, W), F32),
        scratch_shapes=[pltpu.VMEM((1, W), F32)],
        compiler_params=_params("parallel", "arbitrary"), name="cumsum")(x)


_NT = (((1,), (1,)), ((), ()))
_TN = (((0,), (0,)), ((), ()))


def _fox_prompt_body(q_ref, k_ref, v_ref, c_ref, o_ref, m_sc, l_sc, acc_sc, *, tq):
    qi = pl.program_id(1)
    q = q_ref[...]
    m_sc[...] = jnp.full(m_sc.shape, -jnp.inf, F32)
    l_sc[...] = jnp.zeros(l_sc.shape, F32)
    acc_sc[...] = jnp.zeros(acc_sc.shape, F32)
    q0 = pl.multiple_of(qi * tq, tq)
    cb = c_ref[:, pl.ds(q0, LANES)][:, 0:1]

    def step(j, masked):
        s0 = pl.multiple_of(j * tq, tq)
        k = k_ref[pl.ds(s0, tq), :]
        v = v_ref[pl.ds(s0, tq), :]
        s = lax.dot_general(q, k, _NT, preferred_element_type=F32)
        s = s + (cb - c_ref[:, pl.ds(s0, tq)])
        if masked:
            r = lax.broadcasted_iota(jnp.int32, (tq, tq), 0)
            c = lax.broadcasted_iota(jnp.int32, (tq, tq), 1)
            s = jnp.where(c <= r, s, -jnp.inf)
        m_prev = m_sc[...]
        m_new = jnp.maximum(m_prev, jnp.max(s, axis=1, keepdims=True))
        alpha = jnp.exp(m_prev - m_new)
        p = jnp.exp(s - m_new)
        l_sc[...] = alpha * l_sc[...] + jnp.sum(p, axis=1, keepdims=True)
        acc_sc[...] = alpha * acc_sc[...] + jnp.dot(p.astype(BF16), v, preferred_element_type=F32)
        m_sc[...] = m_new

    def body(j, carry):
        step(j, False)
        return carry

    lax.fori_loop(0, qi, body, 0)
    step(qi, True)
    o_ref[...] = (acc_sc[...] / l_sc[...]).astype(o_ref.dtype)


def _fox_prompt(q16, k16, v16, ct, T):
    M, W = q16.shape
    H = W // FOX_HD
    tq = _pick(T, 512, LANES)
    return pl.pallas_call(
        functools.partial(_fox_prompt_body, tq=tq), grid=(H, T // tq),
        in_specs=[pl.BlockSpec((tq, FOX_HD), lambda h, i: (i, h)),
                  pl.BlockSpec((T, FOX_HD), lambda h, i: (0, h)),
                  pl.BlockSpec((T, FOX_HD), lambda h, i: (0, h)),
                  pl.BlockSpec((None, 1, T), lambda h, i: (h, 0, 0))],
        out_specs=pl.BlockSpec((tq, FOX_HD), lambda h, i: (i, h)),
        out_shape=jax.ShapeDtypeStruct((M, W), BF16),
        scratch_shapes=[pltpu.VMEM((tq, 1), F32), pltpu.VMEM((tq, 1), F32),
                        pltpu.VMEM((tq, FOX_HD), F32)],
        compiler_params=_params("parallel", "arbitrary"), name="fox_prompt")(q16, k16, v16, ct)


def _fox_sample_body(q_ref, kn_ref, vn_ref, kp_ref, vp_ref, c_ref, fo_any, o_ref, *, P, ST):
    del fo_any
    q = q_ref[...]
    kp = kp_ref[...].astype(BF16)
    vp = vp_ref[...].astype(BF16)
    c = c_ref[...]
    cb = c[:, P - 1:P]
    sp = lax.dot_general(q, kp, _NT, preferred_element_type=F32) + (cb - c[:, :P])
    sn = lax.dot_general(q, kn_ref[...], _NT, preferred_element_type=F32) + (cb - c[:, P:])
    r = lax.broadcasted_iota(jnp.int32, (ST, ST), 0)
    cc = lax.broadcasted_iota(jnp.int32, (ST, ST), 1)
    sn = jnp.where(cc <= r, sn, -jnp.inf)
    m = jnp.maximum(jnp.max(sp, axis=1, keepdims=True), jnp.max(sn, axis=1, keepdims=True))
    pp = jnp.exp(sp - m)
    pn = jnp.exp(sn - m)
    l = jnp.sum(pp, axis=1, keepdims=True) + jnp.sum(pn, axis=1, keepdims=True)
    o = (jnp.dot(pp.astype(BF16), vp, preferred_element_type=F32)
         + jnp.dot(pn.astype(BF16), vn_ref[...], preferred_element_type=F32))
    o_ref[...] = (o / l).astype(o_ref.dtype)


def _fox_sample(fo, q16, k16, v16, cache_k, cache_v, ct, layer, T, SB, ST):
    M, W = q16.shape
    H = W // FOX_HD
    P = cache_k.shape[2]
    r0 = T // ST
    new = pl.BlockSpec((ST, FOX_HD), lambda b, h: (r0 + b, h))
    past = pl.BlockSpec((None, None, P, FOX_HD), lambda b, h: (layer, b, 0, h))
    return pl.pallas_call(
        functools.partial(_fox_sample_body, P=P, ST=ST), grid=(SB, H),
        in_specs=[new, new, new, past, past,
                  pl.BlockSpec((None, None, 1, P + ST), lambda b, h: (b, h, 0, 0)),
                  pl.BlockSpec(memory_space=pl.ANY)],
        out_specs=new,
        out_shape=jax.ShapeDtypeStruct((M, W), BF16),
        input_output_aliases={6: 0},
        compiler_params=_params("parallel", "arbitrary"), name="fox_sample")(
            q16, k16, v16, cache_k, cache_v, ct, fo)


def _gla_body(q_ref, k_ref, v_ref, r_ref, la_ref, s0_ref, g_ref, go_any, o_ref, sout_ref,
              S_sc, b_sc, q_sc, *, C, NH, DK, DV, nc):
    del go_any
    c = pl.program_id(1)

    @pl.when(c == 0)
    def _():
        S_sc[...] = s0_ref[...]

    b = _row_cumsum(la_ref[...])
    b_sc[...] = b
    q = q_ref[...] * (DK ** -0.5)
    q_sc[...] = q
    bl = b[C - 1:C, :]
    qe = (q * jnp.exp(b)).astype(BF16)
    kdec = (k_ref[...] * jnp.exp(bl - b)).astype(BF16)
    ebl = jnp.exp(bl)
    v16 = v_ref[...].astype(BF16)

    srow = lax.broadcasted_iota(jnp.int32, (C, DK), 0)
    lane = lax.broadcasted_iota(jnp.int32, (C, C), 1)

    def tbody(t8, ats):
        t0 = pl.multiple_of(t8 * SUBLANES, SUBLANES)
        b8 = b_sc[pl.ds(t0, SUBLANES), :]
        q8 = q_sc[pl.ds(t0, SUBLANES), :]
        ats = list(ats)
        for r in range(SUBLANES):
            t = t0 + r
            mask = srow <= t
            lm = lane == t
            for h in range(NH):
                sl = slice(h * DK, (h + 1) * DK)
                bt = b8[r:r + 1, sl]
                qt = q8[r:r + 1, sl]
                e = jnp.exp(jnp.where(mask, bt - b_sc[:, sl], -jnp.inf))
                w = e * (qt * k_ref[:, sl])
                col = jnp.sum(w, axis=1, keepdims=True)
                ats[h] = jnp.where(lm, col, ats[h])
        return tuple(ats)

    ats = lax.fori_loop(0, C // SUBLANES, tbody,
                        tuple(jnp.zeros((C, C), F32) for _ in range(NH)))

    g = g_ref[...]
    for h in range(NH):
        sl = slice(h * DK, (h + 1) * DK)
        vs = slice(h * DV, (h + 1) * DV)
        S = S_sc[h]
        o = (jnp.dot(qe[:, sl], S.astype(BF16), preferred_element_type=F32)
             + lax.dot_general(ats[h].astype(BF16), v16[:, vs], _TN, preferred_element_type=F32))
        r = r_ref[:, vs]
        o = _rms(o) * g * (r * (1.0 / (1.0 + jnp.exp(-r))))
        o_ref[:, vs] = o.astype(o_ref.dtype)
        dm = jnp.transpose(jnp.broadcast_to(ebl[:, sl], (DK, DK)))
        decay = jnp.concatenate([dm] * (DV // DK), axis=1)
        S_sc[h] = decay * S + lax.dot_general(kdec[:, sl], v16[:, vs], _TN, preferred_element_type=F32)

    @pl.when(c == nc - 1)
    def _():
        sout_ref[...] = S_sc[...]


def _gla(rest, la, s0, g, go, *, row0, nseq, nc, C, DK, DV):
    M = rest.shape[0]
    NH = GLA_HEADS
    KW, VW = NH * DK, NH * DV
    r0 = row0 // C
    assert row0 % C == 0 and VW == 2 * KW

    def rows(col):
        return lambda s, c: (r0 + s * nc + c, col)

    in_specs = [pl.BlockSpec((C, KW), rows(0)), pl.BlockSpec((C, KW), rows(1)),
                pl.BlockSpec((C, VW), rows(1)), pl.BlockSpec((C, VW), rows(2)),
                pl.BlockSpec((C, KW), rows(0)),
                pl.BlockSpec((None, NH, DK, DV), lambda s, c: (s, 0, 0, 0)),
                pl.BlockSpec((1, DV), lambda s, c: (0, 0))]
    args = [rest, rest, rest, rest, la, s0, g.reshape(1, DV)]
    aliases = {}
    if go is not None:
        in_specs.append(pl.BlockSpec(memory_space=pl.ANY))
        args.append(go)
        aliases = {7: 0}
        body = _gla_body
    else:
        body = functools.partial(_gla_body_noalias)
    return pl.pallas_call(
        functools.partial(body, C=C, NH=NH, DK=DK, DV=DV, nc=nc), grid=(nseq, nc),
        in_specs=in_specs,
        out_specs=[pl.BlockSpec((C, VW), rows(0)),
                   pl.BlockSpec((None, NH, DK, DV), lambda s, c: (s, 0, 0, 0))],
        out_shape=[jax.ShapeDtypeStruct((M, VW), BF16),
                   jax.ShapeDtypeStruct((nseq, NH, DK, DV), F32)],
        scratch_shapes=[pltpu.VMEM((NH, DK, DV), F32), pltpu.VMEM((C, KW), F32),
                        pltpu.VMEM((C, KW), F32)],
        input_output_aliases=aliases,
        compiler_params=_params("parallel", "arbitrary"), name="gla")(*args)


def _gla_body_noalias(q_ref, k_ref, v_ref, r_ref, la_ref, s0_ref, g_ref, o_ref, sout_ref,
                      S_sc, b_sc, q_sc, **kw):
    _gla_body(q_ref, k_ref, v_ref, r_ref, la_ref, s0_ref, g_ref, None, o_ref, sout_ref,
              S_sc, b_sc, q_sc, **kw)


def _pool_body(*refs, bm, pos0, GC, aliased):
    if aliased:
        p_ref, hist_ref, w_ref, sp_ref, _, o_ref, hout_ref, xa_sc = refs
    else:
        p_ref, hist_ref, w_ref, sp_ref, o_ref, hout_ref, xa_sc = refs
    i = pl.program_id(1)

    @pl.when(i == 0)
    def _():
        xa_sc[0:POOL_CARRY, :] = hist_ref[...]

    xa_sc[POOL_CARRY:POOL_CARRY + bm, :] = p_ref[...]
    pos = pos0 + i * bm + lax.broadcasted_iota(jnp.int32, (bm, 1), 0)
    for gi, w in enumerate(POOL_WINDOWS):
        cs = slice(gi * GC, (gi + 1) * GC)
        x = xa_sc[POOL_CARRY:POOL_CARRY + bm, cs]
        tot = x
        for d in range(1, w):
            tot = tot + xa_sc[POOL_CARRY - d:POOL_CARRY - d + bm, cs]
        cnt = jnp.minimum(pos + 1, w).astype(F32)
        dd = (tot / cnt - x).astype(BF16)
        y = jnp.dot(dd, w_ref[gi], preferred_element_type=F32) * sp_ref[:, cs]
        o_ref[:, cs] = y.astype(o_ref.dtype)
    last = xa_sc[bm:bm + POOL_CARRY, :]
    hout_ref[...] = last
    xa_sc[0:POOL_CARRY, :] = last


def _pool(rest, col, hist, w16, sp, po, *, row0, nseq, nb, bm, pos0):
    M = rest.shape[0]
    G, GC, _ = w16.shape
    PW = G * GC
    r0 = row0 // bm
    assert row0 % bm == 0 and bm >= POOL_CARRY
    in_specs = [pl.BlockSpec((bm, PW), lambda s, i: (r0 + s * nb + i, col)),
                pl.BlockSpec((None, POOL_CARRY, PW), lambda s, i: (s, 0, 0)),
                pl.BlockSpec((G, GC, GC), lambda s, i: (0, 0, 0)),
                pl.BlockSpec((1, PW), lambda s, i: (0, 0))]
    args = [rest, hist, w16, sp.reshape(1, PW)]
    aliases = {}
    if po is not None:
        in_specs.append(pl.BlockSpec(memory_space=pl.ANY))
        args.append(po)
        aliases = {4: 0}
    return pl.pallas_call(
        functools.partial(_pool_body, bm=bm, pos0=pos0, GC=GC, aliased=po is not None),
        grid=(nseq, nb), in_specs=in_specs,
        out_specs=[pl.BlockSpec((bm, PW), lambda s, i: (r0 + s * nb + i, 0)),
                   pl.BlockSpec((None, POOL_CARRY, PW), lambda s, i: (s, 0, 0))],
        out_shape=[jax.ShapeDtypeStruct((M, PW), BF16),
                   jax.ShapeDtypeStruct((nseq, POOL_CARRY, PW), F32)],
        scratch_shapes=[pltpu.VMEM((POOL_CARRY + bm, PW), F32)],
        input_output_aliases=aliases,
        compiler_params=_params("parallel", "arbitrary"), name="pool")(*args)


def kernel(x_prompt, x_sample, cache_fox_k, cache_fox_v, cache_fox_logf, state_gla, state_pool, g_pre_mix, w_in, b_fox_f, w_gla_a2, b_gla_a, g_gla_norm, w_pool, s_pool, w_out, g_post_mix, g_pre_mlp, w_up, w_down, g_post_mlp):
    B, T, D = x_prompt.shape
    SB, ST, _ = x_sample.shape
    L = w_in.shape[0]
    P = cache_fox_k.shape[2]
    assert B == 1
    FW = D // 2
    H = FW // FOX_HD
    GW = D // 4
    DV = GW // GLA_HEADS
    DK = DV // 2
    KW = GLA_HEADS * DK
    RANK = w_gla_a2.shape[1]
    PW = D - FW - GW
    GC = PW // len(POOL_WINDOWS)
    MS = SB * ST
    M = T + MS
    assert H + RANK <= LANES and 2 * KW + 2 * GW == 3 * PW and PW % LANES == 0

    o_fq, o_fk, o_fv, o_ff = 0, FW, 2 * FW, 3 * FW
    o_gq = o_ff + H
    o_ga = o_gq + 2 * KW + GW
    o_gr = o_ga + RANK

    cache_k = cache_fox_k.reshape(L, SB, P, FW)
    cache_v = cache_fox_v.reshape(L, SB, P, FW)
    past_logf = jnp.pad(cache_fox_logf.astype(F32), ((0, 0), (0, 0), (0, 0), (0, LANES - H)))
    zero_state = jnp.zeros((1, GLA_HEADS, DK, DV), F32)
    zero_hist = jnp.zeros((1, POOL_CARRY, PW), F32)
    sample_hist = jnp.pad(state_pool, ((0, 0), (0, 0), (POOL_CARRY - POOL_HIST, 0), (0, 0)))

    h = jnp.concatenate([x_prompt.reshape(T, D), x_sample.reshape(MS, D)], axis=0)
    a = _norm_first(h, g_pre_mix[0])
    outs_p = [[] for _ in range(5)]
    outs_s = [[] for _ in range(5)]
    for l in range(L):
        wi = w_in[l]
        w_q = (wi[:, o_fq:o_fk] * (FOX_HD ** -0.5)).astype(BF16)
        w_k = wi[:, o_fk:o_fv].astype(BF16)
        w_v = wi[:, o_fv:o_ff].astype(BF16)
        w_rest = jnp.concatenate([wi[:, o_gq:o_ga], wi[:, o_gr:]], axis=1).astype(BF16)
        w_tail = jnp.concatenate([wi[:, o_ff:o_gq], wi[:, o_ga:o_gr],
                                  jnp.zeros((D, LANES - H - RANK), F32)], axis=1).astype(BF16)
        bfox_pad = jnp.pad(b_fox_f[l].astype(F32), (0, LANES - H)).reshape(1, LANES)
        w2_pad = jnp.pad(w_gla_a2[l].astype(F32), ((H, LANES - H - RANK), (0, 0)))

        (q16,) = _matmul(a, w_q, [BF16], name="proj_q")
        k32, k16 = _matmul(a, w_k, [F32, BF16], name="proj_k")
        v32, v16 = _matmul(a, w_v, [F32, BF16], name="proj_v")
        (rest,) = _matmul(a, w_rest, [F32], name="proj_rest")
        (tail,) = _matmul(a, w_tail, [F32], name="proj_tail")
        logf, la = _gates(tail, bfox_pad, w2_pad, b_gla_a[l].reshape(1, KW))

        c_p = _cumsum(logf[:T].reshape(1, T, LANES), 1024)
        ct_p = jnp.transpose(c_p[0, :, :H]).reshape(H, 1, T)
        fo = _fox_prompt(q16, k16, v16, ct_p, T)
        lf_all = jnp.concatenate([past_logf[l], logf[T:].reshape(SB, ST, LANES)], axis=1)
        c_s = _cumsum(lf_all, P + ST)
        ct_s = jnp.transpose(c_s[:, :, :H], (0, 2, 1)).reshape(SB, H, 1, P + ST)
        fo = _fox_sample(fo, q16, k16, v16, cache_k, cache_v, ct_s, l, T, SB, ST)

        go, gla_p = _gla(rest, la, zero_state, g_gla_norm[l], None,
                         row0=0, nseq=1, nc=T // GLA_CHUNK, C=GLA_CHUNK, DK=DK, DV=DV)
        go, gla_s = _gla(rest, la, state_gla[l], g_gla_norm[l], go,
                         row0=T, nseq=SB, nc=1, C=ST, DK=DK, DV=DV)

        w_pool16 = w_pool[l].astype(BF16)
        pcol = (2 * KW + 2 * GW) // PW
        bp = _pick(T, 512)
        po, hist_p = _pool(rest, pcol, zero_hist, w_pool16, s_pool[l], None,
                           row0=0, nseq=1, nb=T // bp, bm=bp, pos0=0)
        po, hist_s = _pool(rest, pcol, sample_hist[l], w_pool16, s_pool[l], po,
                           row0=T, nseq=SB, nb=1, bm=ST, pos0=P)

        y = _matmul_cat([fo, go, po], w_out[l].astype(BF16), name="proj_out")
        h, c = _norm_mid(h, y, g_post_mix[l], g_pre_mlp[l])
        (hid,) = _matmul(c, w_up[l].astype(BF16), [BF16], name="mlp_up", relu2=True)
        f = _matmul_kacc(hid, w_down[l].astype(BF16), name="mlp_down")
        if l + 1 < L:
            h, a = _norm_mid(h, f, g_post_mlp[l], g_pre_mix[l + 1])
        else:
            h = _norm_last(h, f, g_post_mlp[l])

        outs_p[0].append(k32[:T].reshape(1, T, H, FOX_HD))
        outs_p[1].append(v32[:T].reshape(1, T, H, FOX_HD))
        outs_p[2].append(logf[:T, :H].reshape(1, T, H))
        outs_p[3].append(gla_p)
        outs_p[4].append(hist_p[:, POOL_CARRY - POOL_HIST:])
        outs_s[0].append(k32[T:].reshape(SB, ST, H, FOX_HD))
        outs_s[1].append(v32[T:].reshape(SB, ST, H, FOX_HD))
        outs_s[2].append(logf[T:, :H].reshape(SB, ST, H))
        outs_s[3].append(gla_s)
        outs_s[4].append(hist_s[:, POOL_CARRY - POOL_HIST:])

    return (h[:T].reshape(1, T, D), h[T:].reshape(SB, ST, D),
            *[jnp.stack(o) for o in outs_p], *[jnp.stack(o) for o in outs_s])
```

```python
import functools

import jax
import jax.numpy as jnp
from jax import lax
from jax.experimental import pallas as pl
from jax.experimental.pallas import tpu as pltpu

F32 = jnp.float32
BF16 = jnp.bfloat16

RMS_EPS = 1e-6
FOX_HD = 128
GLA_HEADS = 4
GLA_TAU = 16.0
GLA_CHUNK = 64
POOL_WINDOWS = (2, 4, 8, 16)
POOL_HIST = 15
POOL_CARRY = 16
LANES = 128
SUBLANES = 8
ONES_ROWS = 16
LOG2E = 1.4426950408889634
VMEM_LIMIT = 56 * 1024 * 1024


def _params(*sem):
    return pltpu.CompilerParams(dimension_semantics=sem, vmem_limit_bytes=VMEM_LIMIT)


def _pick(n, pref, mult=8):
    b = min(pref, n)
    b -= b % mult
    while b > 0 and n % b:
        b -= mult
    assert b > 0, (n, pref, mult)
    return b


def _rms(x):
    return x * lax.rsqrt(jnp.mean(x * x, axis=-1, keepdims=True) + RMS_EPS)


def _log_sigmoid(x):
    return jnp.minimum(x, 0.0) - jnp.log1p(jnp.exp(-jnp.abs(x)))


def _norm_first_body(h_ref, g_ref, a_ref):
    a_ref[...] = (_rms(h_ref[...]) * g_ref[...]).astype(a_ref.dtype)


def _norm_mid_body(h_ref, y_ref, gy_ref, g_ref, h_out, a_out):
    h = h_ref[...] + _rms(y_ref[...]) * gy_ref[...]
    h_out[...] = h
    a_out[...] = (_rms(h) * g_ref[...]).astype(a_out.dtype)


def _norm_last_body(h_ref, y_ref, gy_ref, h_out):
    h_out[...] = h_ref[...] + _rms(y_ref[...]) * gy_ref[...]


def _norm_first(h, g):
    M, D = h.shape
    bm = _pick(M, 256)
    row = pl.BlockSpec((bm, D), lambda i: (i, 0))
    vec = pl.BlockSpec((1, D), lambda i: (0, 0))
    return pl.pallas_call(
        _norm_first_body, grid=(M // bm,), in_specs=[row, vec], out_specs=row,
        out_shape=jax.ShapeDtypeStruct((M, D), BF16),
        compiler_params=_params("parallel"), name="norm_first")(h, g.reshape(1, D))


def _norm_mid(h, y, gy, g):
    M, D = h.shape
    bm = _pick(M, 256)
    row = pl.BlockSpec((bm, D), lambda i: (i, 0))
    vec = pl.BlockSpec((1, D), lambda i: (0, 0))
    return pl.pallas_call(
        _norm_mid_body, grid=(M // bm,), in_specs=[row, row, vec, vec], out_specs=[row, row],
        out_shape=[jax.ShapeDtypeStruct((M, D), F32), jax.ShapeDtypeStruct((M, D), BF16)],
        compiler_params=_params("parallel"), name="norm_mid")(h, y, gy.reshape(1, D), g.reshape(1, D))


def _norm_last(h, y, gy):
    M, D = h.shape
    bm = _pick(M, 256)
    row = pl.BlockSpec((bm, D), lambda i: (i, 0))
    vec = pl.BlockSpec((1, D), lambda i: (0, 0))
    return pl.pallas_call(
        _norm_last_body, grid=(M // bm,), in_specs=[row, row, vec], out_specs=row,
        out_shape=jax.ShapeDtypeStruct((M, D), F32),
        compiler_params=_params("parallel"), name="norm_last")(h, y, gy.reshape(1, D))


def _mm_body(x_ref, w_ref, *o_refs, relu2, scale):
    acc = jnp.dot(x_ref[...], w_ref[...], preferred_element_type=F32)
    if relu2:
        acc = jnp.square(jnp.maximum(acc, 0.0))
    if scale is not None:
        acc = acc * scale
    for o in o_refs:
        o[...] = acc.astype(o.dtype)


def _matmul(x, w, out_dtypes, *, name, relu2=False, scale=None, col0=0, ncols=None, bm=1024, bn=1024):
    M, K = x.shape
    N = w.shape[1] - col0 if ncols is None else ncols
    bm = _pick(M, bm)
    bn = _pick(N, bn, LANES)
    assert col0 % bn == 0
    c0 = col0 // bn
    outs = pl.pallas_call(
        functools.partial(_mm_body, relu2=relu2, scale=scale), grid=(M // bm, N // bn),
        in_specs=[pl.BlockSpec((bm, K), lambda i, j: (i, 0)),
                  pl.BlockSpec((K, bn), lambda i, j: (0, c0 + j))],
        out_specs=[pl.BlockSpec((bm, bn), lambda i, j: (i, j)) for _ in out_dtypes],
        out_shape=[jax.ShapeDtypeStruct((M, N), dt) for dt in out_dtypes],
        compiler_params=_params("parallel", "arbitrary"), name=name)(x, w)
    return outs


def _mm_cat_body(*refs, splits):
    x_refs, w_ref, o_ref = refs[:-2], refs[-2], refs[-1]
    acc = None
    off = 0
    for x_ref, kk in zip(x_refs, splits):
        part = jnp.dot(x_ref[...], w_ref[off:off + kk, :], preferred_element_type=F32)
        acc = part if acc is None else acc + part
        off += kk
    o_ref[...] = acc


def _matmul_cat(xs, w, *, name, bm=1024, bn=1024):
    M = xs[0].shape[0]
    K, N = w.shape
    splits = tuple(x.shape[1] for x in xs)
    assert sum(splits) == K
    bm = _pick(M, bm)
    bn = _pick(N, bn, LANES)
    in_specs = [pl.BlockSpec((bm, kk), lambda i, j: (i, 0)) for kk in splits]
    in_specs.append(pl.BlockSpec((K, bn), lambda i, j: (0, j)))
    return pl.pallas_call(
        functools.partial(_mm_cat_body, splits=splits), grid=(M // bm, N // bn),
        in_specs=in_specs, out_specs=pl.BlockSpec((bm, bn), lambda i, j: (i, j)),
        out_shape=jax.ShapeDtypeStruct((M, N), F32),
        compiler_params=_params("parallel", "arbitrary"), name=name)(*xs, w)


def _mm_kacc_body(x_ref, w_ref, o_ref, *, bn):
    k = pl.program_id(1)

    @pl.when(k == 0)
    def _():
        o_ref[...] = jnp.zeros(o_ref.shape, F32)

    x = x_ref[...]
    for n0 in range(0, o_ref.shape[1], bn):
        o_ref[:, n0:n0 + bn] += jnp.dot(x, w_ref[:, n0:n0 + bn], preferred_element_type=F32)


def _matmul_kacc(x, w, *, name, bm=512, bk=1024):
    M, K = x.shape
    N = w.shape[1]
    bm = _pick(M, bm)
    bk = _pick(K, bk, LANES)
    return pl.pallas_call(
        functools.partial(_mm_kacc_body, bn=_pick(N, 512, LANES)), grid=(M // bm, K // bk),
        in_specs=[pl.BlockSpec((bm, bk), lambda i, k: (i, k)),
                  pl.BlockSpec((bk, N), lambda i, k: (k, 0))],
        out_specs=pl.BlockSpec((bm, N), lambda i, k: (i, 0)),
        out_shape=jax.ShapeDtypeStruct((M, N), F32),
        compiler_params=_params("parallel", "arbitrary"), name=name)(x, w)


def _gates_body(t_ref, bf_ref, w2_ref, bg_ref, logf_ref, la_ref):
    t = t_ref[...]
    logf_ref[...] = _log_sigmoid(t + bf_ref[...])
    zg = jnp.dot(t, w2_ref[...], preferred_element_type=F32) + bg_ref[...]
    la_ref[...] = _log_sigmoid(zg) * (1.0 / GLA_TAU)


def _gates(tail, bfox_pad, w2_pad, bgla):
    M = tail.shape[0]
    KW = w2_pad.shape[1]
    bm = _pick(M, 512)
    return pl.pallas_call(
        _gates_body, grid=(M // bm,),
        in_specs=[pl.BlockSpec((bm, LANES), lambda i: (i, 0)),
                  pl.BlockSpec((1, LANES), lambda i: (0, 0)),
                  pl.BlockSpec((LANES, KW), lambda i: (0, 0)),
                  pl.BlockSpec((1, KW), lambda i: (0, 0))],
        out_specs=[pl.BlockSpec((bm, LANES), lambda i: (i, 0)),
                   pl.BlockSpec((bm, KW), lambda i: (i, 0))],
        out_shape=[jax.ShapeDtypeStruct((M, LANES), F32), jax.ShapeDtypeStruct((M, KW), F32)],
        compiler_params=_params("parallel"), name="gates")(tail, bfox_pad, w2_pad, bgla)


def _row_cumsum(x):
    n = x.shape[0]
    row = lax.broadcasted_iota(jnp.int32, x.shape, 0)
    sh = 1
    while sh < n:
        x = x + jnp.where(row >= sh, pltpu.roll(x, sh, axis=0), 0.0)
        sh *= 2
    return x


def _cumsum_body(x_ref, c_ref, carry_sc):
    @pl.when(pl.program_id(1) == 0)
    def _():
        carry_sc[...] = jnp.zeros(carry_sc.shape, F32)

    c = _row_cumsum(x_ref[...]) + carry_sc[...]
    c_ref[...] = c
    n = c.shape[0]
    carry_sc[...] = c[n - 1:n, :]


def _cumsum(x, bl):
    S, L, W = x.shape
    bl = _pick(L, bl)
    spec = pl.BlockSpec((None, bl, W), lambda s, i: (s, i, 0))
    return pl.pallas_call(
        _cumsum_body, grid=(S, L // bl), in_specs=[spec], out_specs=spec,
        out_shape=jax.ShapeDtypeStruct((S, L, W), F32),
        scratch_shapes=[pltpu.VMEM((1, W), F32)],
        compiler_params=_params("parallel", "arbitrary"), name="cumsum")(x)


_NT = (((1,), (1,)), ((), ()))
_TN = (((0,), (0,)), ((), ()))


def _fox_prompt_body(q_ref, k_ref, vt_ref, b_ref, o_ref, acc_sc, st_sc, *, tq, tk, nsub):
    qi = pl.program_id(1)
    hd = k_ref.shape[1]
    bq = nsub * tq
    nkb = bq // tk
    ones = jnp.ones((ONES_ROWS, tk), BF16)
    rep = tq // LANES

    def scores(j, sub, mask_off):
        s0 = pl.multiple_of(j * tk, tk)
        q = q_ref[sub * tq:(sub + 1) * tq, :]
        k = k_ref[pl.ds(s0, tk), :]
        st = lax.dot_general(k, q, _NT, preferred_element_type=F32)
        b = b_ref[pl.ds(s0, tk), :]
        st = st + jnp.concatenate([b] * rep, axis=1)
        if mask_off is not None:
            key = lax.broadcasted_iota(jnp.int32, (tk, tq), 0) + mask_off
            qry = lax.broadcasted_iota(jnp.int32, (tk, tq), 1)
            st = jnp.where(key <= qry, st, -jnp.inf)
        return st

    def mask(st, mask_off):
        key = lax.broadcasted_iota(jnp.int32, (tk, tq), 0) + mask_off
        qry = lax.broadcasted_iota(jnp.int32, (tk, tq), 1)
        return jnp.where(key <= qry, st, -jnp.inf)

    def scores(j, sub):
        s0 = pl.multiple_of(j * tk, tk)
        q = q_ref[sub * tq:(sub + 1) * tq, :]
        k = k_ref[pl.ds(s0, tk), :]
        st = lax.dot_general(k, q, _NT, preferred_element_type=F32)
        b = b_ref[pl.ds(s0, tk), :]
        return st + jnp.concatenate([b] * rep, axis=1)

    def accumulate(j, sub, st, m_blk, m_prev):
        s0 = pl.multiple_of(j * tk, tk)
        m_new = jnp.maximum(m_prev, m_blk)
        alpha = jnp.exp2(m_prev - m_new)
        pt = jnp.exp2(st - m_new).astype(BF16)
        vt = jnp.concatenate([vt_ref[:, pl.ds(s0, tk)], ones], axis=0)
        acc_sc[sub] = alpha * acc_sc[sub] + jnp.dot(vt, pt, preferred_element_type=F32)
        return m_new

    def issue(j, slot):
        mb = []
        for sub in range(nsub):
            st = scores(j, sub)
            st_sc[slot, sub] = st
            mb.append(jnp.max(st, axis=0, keepdims=True))
        return tuple(mb)

    acc_sc[...] = jnp.zeros(acc_sc.shape, F32)
    m0 = tuple(jnp.full((1, tq), -jnp.inf, F32) for _ in range(nsub))
    n = nkb * qi

    def body(jj, carry):
        ms, mb = carry
        for slot in range(2):
            j = 2 * jj + slot
            mb_next = issue(j + 1, 1 - slot)
            ms = tuple(accumulate(j, sub, st_sc[slot, sub], mb[sub], ms[sub]) for sub in range(nsub))
            mb = mb_next
        return ms, mb

    assert nkb % 2 == 0
    ms, _ = lax.fori_loop(0, n // 2, body, (m0, issue(0, 0)))
    ms = list(ms)
    for d in range(nkb):
        for sub in range(nsub):
            k_lo, k_hi = d * tk, (d + 1) * tk - 1
            q_lo, q_hi = sub * tq, (sub + 1) * tq - 1
            if k_lo > q_hi:
                continue
            st = st_sc[0, sub] if d == 0 else scores(n + d, sub)
            if k_hi > q_lo:
                st = mask(st, k_lo - q_lo)
            ms[sub] = accumulate(n + d, sub, st, jnp.max(st, axis=0, keepdims=True), ms[sub])
    for sub in range(nsub):
        acc = acc_sc[sub]
        o = acc[:hd, :] / acc[hd:hd + 1, :]
        o_ref[sub * tq:(sub + 1) * tq, :] = jnp.transpose(o).astype(o_ref.dtype)


FOX_TQ, FOX_TK, FOX_NSUB = 512, 512, 2


def _fox_prompt(q16, k16, vt16, brep, T):
    M, W = q16.shape
    H = W // FOX_HD
    tq, tk, nsub = FOX_TQ, FOX_TK, FOX_NSUB
    bq = nsub * tq
    assert T % bq == 0 and bq % tk == 0
    return pl.pallas_call(
        functools.partial(_fox_prompt_body, tq=tq, tk=tk, nsub=nsub), grid=(H, T // bq),
        in_specs=[pl.BlockSpec((bq, FOX_HD), lambda h, i: (i, h)),
                  pl.BlockSpec((T, FOX_HD), lambda h, i: (0, h)),
                  pl.BlockSpec((FOX_HD, T), lambda h, i: (h, 0)),
                  pl.BlockSpec((None, T, LANES), lambda h, i: (h, 0, 0))],
        out_specs=pl.BlockSpec((bq, FOX_HD), lambda h, i: (i, h)),
        out_shape=jax.ShapeDtypeStruct((M, W), BF16),
        scratch_shapes=[pltpu.VMEM((nsub, FOX_HD + ONES_ROWS, tq), F32),
                        pltpu.VMEM((2, nsub, tk, tq), F32)],
        compiler_params=_params("parallel", "arbitrary"), name="fox_prompt")(q16, k16, vt16, brep)


def _fox_sample_body(q_ref, kn_ref, vn_ref, kp_ref, vp_ref, c_ref, fo_any, o_ref, *, P, ST):
    del fo_any
    q = q_ref[...]
    kp = kp_ref[...].astype(BF16)
    vp = vp_ref[...].astype(BF16)
    c = c_ref[...]
    cb = c[:, P - 1:P]
    sp = lax.dot_general(q, kp, _NT, preferred_element_type=F32) + (cb - c[:, :P]) * LOG2E
    sn = lax.dot_general(q, kn_ref[...], _NT, preferred_element_type=F32) + (cb - c[:, P:]) * LOG2E
    r = lax.broadcasted_iota(jnp.int32, (ST, ST), 0)
    cc = lax.broadcasted_iota(jnp.int32, (ST, ST), 1)
    sn = jnp.where(cc <= r, sn, -jnp.inf)
    m = jnp.maximum(jnp.max(sp, axis=1, keepdims=True), jnp.max(sn, axis=1, keepdims=True))
    pp = jnp.exp2(sp - m)
    pn = jnp.exp2(sn - m)
    l = jnp.sum(pp, axis=1, keepdims=True) + jnp.sum(pn, axis=1, keepdims=True)
    o = (jnp.dot(pp.astype(BF16), vp, preferred_element_type=F32)
         + jnp.dot(pn.astype(BF16), vn_ref[...], preferred_element_type=F32))
    o_ref[...] = (o / l).astype(o_ref.dtype)


def _fox_sample(fo, q16, k16, v16, cache_k, cache_v, ct, layer, T, SB, ST):
    M, W = q16.shape
    H = W // FOX_HD
    P = cache_k.shape[2]
    r0 = T // ST
    new = pl.BlockSpec((ST, FOX_HD), lambda b, h: (r0 + b, h))
    past = pl.BlockSpec((None, None, P, FOX_HD), lambda b, h: (layer, b, 0, h))
    return pl.pallas_call(
        functools.partial(_fox_sample_body, P=P, ST=ST), grid=(SB, H),
        in_specs=[new, new, new, past, past,
                  pl.BlockSpec((None, None, 1, P + ST), lambda b, h: (b, h, 0, 0)),
                  pl.BlockSpec(memory_space=pl.ANY)],
        out_specs=new,
        out_shape=jax.ShapeDtypeStruct((M, W), BF16),
        input_output_aliases={6: 0},
        compiler_params=_params("parallel", "arbitrary"), name="fox_sample")(
            q16, k16, v16, cache_k, cache_v, ct, fo)


def _gla_body(q_ref, k_ref, v_ref, r_ref, la_ref, s0_ref, g_ref, go_any, o_ref, sout_ref,
              S_sc, b_sc, q_sc, *, C, NH, DK, DV, nc):
    del go_any
    c = pl.program_id(1)

    @pl.when(c == 0)
    def _():
        S_sc[...] = s0_ref[...]

    b = _row_cumsum(la_ref[...])
    b_sc[...] = b
    q = q_ref[...] * (DK ** -0.5)
    q_sc[...] = q
    bl = b[C - 1:C, :]
    qe = (q * jnp.exp(b)).astype(BF16)
    kdec = (k_ref[...] * jnp.exp(bl - b)).astype(BF16)
    ebl = jnp.exp(bl)
    v16 = v_ref[...].astype(BF16)

    srow = lax.broadcasted_iota(jnp.int32, (C, DK), 0)
    lane = lax.broadcasted_iota(jnp.int32, (C, C), 1)

    def tbody(t8, ats):
        t0 = pl.multiple_of(t8 * SUBLANES, SUBLANES)
        b8 = b_sc[pl.ds(t0, SUBLANES), :]
        q8 = q_sc[pl.ds(t0, SUBLANES), :]
        ats = list(ats)
        for r in range(SUBLANES):
            t = t0 + r
            mask = srow <= t
            lm = lane == t
            for h in range(NH):
                sl = slice(h * DK, (h + 1) * DK)
                bt = b8[r:r + 1, sl]
                qt = q8[r:r + 1, sl]
                e = jnp.exp(jnp.where(mask, bt - b_sc[:, sl], -jnp.inf))
                w = e * (qt * k_ref[:, sl])
                col = jnp.sum(w, axis=1, keepdims=True)
                ats[h] = jnp.where(lm, col, ats[h])
        return tuple(ats)

    ats = lax.fori_loop(0, C // SUBLANES, tbody,
                        tuple(jnp.zeros((C, C), F32) for _ in range(NH)))

    g = g_ref[...]
    for h in range(NH):
        sl = slice(h * DK, (h + 1) * DK)
        vs = slice(h * DV, (h + 1) * DV)
        S = S_sc[h]
        o = (jnp.dot(qe[:, sl], S.astype(BF16), preferred_element_type=F32)
             + lax.dot_general(ats[h].astype(BF16), v16[:, vs], _TN, preferred_element_type=F32))
        r = r_ref[:, vs]
        o = _rms(o) * g * (r * (1.0 / (1.0 + jnp.exp(-r))))
        o_ref[:, vs] = o.astype(o_ref.dtype)
        dm = jnp.transpose(jnp.broadcast_to(ebl[:, sl], (DK, DK)))
        decay = jnp.concatenate([dm] * (DV // DK), axis=1)
        S_sc[h] = decay * S + lax.dot_general(kdec[:, sl], v16[:, vs], _TN, preferred_element_type=F32)

    @pl.when(c == nc - 1)
    def _():
        sout_ref[...] = S_sc[...]


def _gla(rest, la, s0, g, go, *, row0, nseq, nc, C, DK, DV):
    M = rest.shape[0]
    NH = GLA_HEADS
    KW, VW = NH * DK, NH * DV
    r0 = row0 // C
    assert row0 % C == 0 and VW == 2 * KW

    def rows(col):
        return lambda s, c: (r0 + s * nc + c, col)

    in_specs = [pl.BlockSpec((C, KW), rows(0)), pl.BlockSpec((C, KW), rows(1)),
                pl.BlockSpec((C, VW), rows(1)), pl.BlockSpec((C, VW), rows(2)),
                pl.BlockSpec((C, KW), rows(0)),
                pl.BlockSpec((None, NH, DK, DV), lambda s, c: (s, 0, 0, 0)),
                pl.BlockSpec((1, DV), lambda s, c: (0, 0))]
    args = [rest, rest, rest, rest, la, s0, g.reshape(1, DV)]
    aliases = {}
    if go is not None:
        in_specs.append(pl.BlockSpec(memory_space=pl.ANY))
        args.append(go)
        aliases = {7: 0}
        body = _gla_body
    else:
        body = functools.partial(_gla_body_noalias)
    return pl.pallas_call(
        functools.partial(body, C=C, NH=NH, DK=DK, DV=DV, nc=nc), grid=(nseq, nc),
        in_specs=in_specs,
        out_specs=[pl.BlockSpec((C, VW), rows(0)),
                   pl.BlockSpec((None, NH, DK, DV), lambda s, c: (s, 0, 0, 0))],
        out_shape=[jax.ShapeDtypeStruct((M, VW), BF16),
                   jax.ShapeDtypeStruct((nseq, NH, DK, DV), F32)],
        scratch_shapes=[pltpu.VMEM((NH, DK, DV), F32), pltpu.VMEM((C, KW), F32),
                        pltpu.VMEM((C, KW), F32)],
        input_output_aliases=aliases,
        compiler_params=_params("parallel", "arbitrary"), name="gla")(*args)


def _gla_body_noalias(q_ref, k_ref, v_ref, r_ref, la_ref, s0_ref, g_ref, o_ref, sout_ref,
                      S_sc, b_sc, q_sc, **kw):
    _gla_body(q_ref, k_ref, v_ref, r_ref, la_ref, s0_ref, g_ref, None, o_ref, sout_ref,
              S_sc, b_sc, q_sc, **kw)


def _pool_body(*refs, bm, pos0, GC, aliased):
    if aliased:
        p_ref, hist_ref, w_ref, sp_ref, _, o_ref, hout_ref, xa_sc = refs
    else:
        p_ref, hist_ref, w_ref, sp_ref, o_ref, hout_ref, xa_sc = refs
    i = pl.program_id(1)

    @pl.when(i == 0)
    def _():
        xa_sc[0:POOL_CARRY, :] = hist_ref[...]

    xa_sc[POOL_CARRY:POOL_CARRY + bm, :] = p_ref[...]
    pos = pos0 + i * bm + lax.broadcasted_iota(jnp.int32, (bm, 1), 0)
    for gi, w in enumerate(POOL_WINDOWS):
        cs = slice(gi * GC, (gi + 1) * GC)
        x = xa_sc[POOL_CARRY:POOL_CARRY + bm, cs]
        tot = x
        for d in range(1, w):
            tot = tot + xa_sc[POOL_CARRY - d:POOL_CARRY - d + bm, cs]
        cnt = jnp.minimum(pos + 1, w).astype(F32)
        dd = (tot / cnt - x).astype(BF16)
        y = jnp.dot(dd, w_ref[gi], preferred_element_type=F32) * sp_ref[:, cs]
        o_ref[:, cs] = y.astype(o_ref.dtype)
    last = xa_sc[bm:bm + POOL_CARRY, :]
    hout_ref[...] = last
    xa_sc[0:POOL_CARRY, :] = last


def _pool(rest, col, hist, w16, sp, po, *, row0, nseq, nb, bm, pos0):
    M = rest.shape[0]
    G, GC, _ = w16.shape
    PW = G * GC
    r0 = row0 // bm
    assert row0 % bm == 0 and bm >= POOL_CARRY
    in_specs = [pl.BlockSpec((bm, PW), lambda s, i: (r0 + s * nb + i, col)),
                pl.BlockSpec((None, POOL_CARRY, PW), lambda s, i: (s, 0, 0)),
                pl.BlockSpec((G, GC, GC), lambda s, i: (0, 0, 0)),
                pl.BlockSpec((1, PW), lambda s, i: (0, 0))]
    args = [rest, hist, w16, sp.reshape(1, PW)]
    aliases = {}
    if po is not None:
        in_specs.append(pl.BlockSpec(memory_space=pl.ANY))
        args.append(po)
        aliases = {4: 0}
    return pl.pallas_call(
        functools.partial(_pool_body, bm=bm, pos0=pos0, GC=GC, aliased=po is not None),
        grid=(nseq, nb), in_specs=in_specs,
        out_specs=[pl.BlockSpec((bm, PW), lambda s, i: (r0 + s * nb + i, 0)),
                   pl.BlockSpec((None, POOL_CARRY, PW), lambda s, i: (s, 0, 0))],
        out_shape=[jax.ShapeDtypeStruct((M, PW), BF16),
                   jax.ShapeDtypeStruct((nseq, POOL_CARRY, PW), F32)],
        scratch_shapes=[pltpu.VMEM((POOL_CARRY + bm, PW), F32)],
        input_output_aliases=aliases,
        compiler_params=_params("parallel", "arbitrary"), name="pool")(*args)


def kernel(x_prompt, x_sample, cache_fox_k, cache_fox_v, cache_fox_logf, state_gla, state_pool, g_pre_mix, w_in, b_fox_f, w_gla_a2, b_gla_a, g_gla_norm, w_pool, s_pool, w_out, g_post_mix, g_pre_mlp, w_up, w_down, g_post_mlp):
    B, T, D = x_prompt.shape
    SB, ST, _ = x_sample.shape
    L = w_in.shape[0]
    P = cache_fox_k.shape[2]
    assert B == 1
    FW = D // 2
    H = FW // FOX_HD
    GW = D // 4
    DV = GW // GLA_HEADS
    DK = DV // 2
    KW = GLA_HEADS * DK
    RANK = w_gla_a2.shape[1]
    PW = D - FW - GW
    GC = PW // len(POOL_WINDOWS)
    MS = SB * ST
    M = T + MS
    assert H + RANK <= LANES and 2 * KW + 2 * GW == 3 * PW and PW % LANES == 0

    o_fq, o_fk, o_fv, o_ff = 0, FW, 2 * FW, 3 * FW
    o_gq = o_ff + H
    o_ga = o_gq + 2 * KW + GW
    o_gr = o_ga + RANK

    cache_k = cache_fox_k.reshape(L, SB, P, FW)
    cache_v = cache_fox_v.reshape(L, SB, P, FW)
    past_logf = jnp.pad(cache_fox_logf.astype(F32), ((0, 0), (0, 0), (0, 0), (0, LANES - H)))
    zero_state = jnp.zeros((1, GLA_HEADS, DK, DV), F32)
    zero_hist = jnp.zeros((1, POOL_CARRY, PW), F32)
    sample_hist = jnp.pad(state_pool, ((0, 0), (0, 0), (POOL_CARRY - POOL_HIST, 0), (0, 0)))

    h = jnp.concatenate([x_prompt.reshape(T, D), x_sample.reshape(MS, D)], axis=0)
    a = _norm_first(h, g_pre_mix[0])
    outs_p = [[] for _ in range(5)]
    outs_s = [[] for _ in range(5)]
    for l in range(L):
        wi = w_in[l].astype(BF16)
        w_rest = jnp.concatenate([wi[:, o_gq:o_ga], wi[:, o_gr:]], axis=1)
        w_tail = jnp.concatenate([wi[:, o_ff:o_gq], wi[:, o_ga:o_gr],
                                  jnp.zeros((D, LANES - H - RANK), BF16)], axis=1)
        bfox_pad = jnp.pad(b_fox_f[l].astype(F32), (0, LANES - H)).reshape(1, LANES)
        w2_pad = jnp.pad(w_gla_a2[l].astype(F32), ((H, LANES - H - RANK), (0, 0)))

        (q16,) = _matmul(a, wi, [BF16], name="proj_q", col0=o_fq, ncols=FW,
                         scale=FOX_HD ** -0.5 * LOG2E)
        k32, k16 = _matmul(a, wi, [F32, BF16], name="proj_k", col0=o_fk, ncols=FW)
        v32, v16 = _matmul(a, wi, [F32, BF16], name="proj_v", col0=o_fv, ncols=FW)
        (rest,) = _matmul(a, w_rest, [F32], name="proj_rest")
        (tail,) = _matmul(a, w_tail, [F32], name="proj_tail")
        logf, la = _gates(tail, bfox_pad, w2_pad, b_gla_a[l].reshape(1, KW))

        c_p = _cumsum(logf[:T].reshape(1, T, LANES), 1024)
        brep = jnp.broadcast_to(jnp.transpose(c_p[0, :, :H] * (-LOG2E))[:, :, None], (H, T, LANES))
        vt16 = jnp.transpose(v16[:T])
        fo = _fox_prompt(q16, k16, vt16, brep, T)
        lf_all = jnp.concatenate([past_logf[l], logf[T:].reshape(SB, ST, LANES)], axis=1)
        c_s = _cumsum(lf_all, P + ST)
        ct_s = jnp.transpose(c_s[:, :, :H], (0, 2, 1)).reshape(SB, H, 1, P + ST)
        fo = _fox_sample(fo, q16, k16, v16, cache_k, cache_v, ct_s, l, T, SB, ST)

        go, gla_p = _gla(rest, la, zero_state, g_gla_norm[l], None,
                         row0=0, nseq=1, nc=T // GLA_CHUNK, C=GLA_CHUNK, DK=DK, DV=DV)
        go, gla_s = _gla(rest, la, state_gla[l], g_gla_norm[l], go,
                         row0=T, nseq=SB, nc=1, C=ST, DK=DK, DV=DV)

        w_pool16 = w_pool[l].astype(BF16)
        pcol = (2 * KW + 2 * GW) // PW
        bp = _pick(T, 512)
        po, hist_p = _pool(rest, pcol, zero_hist, w_pool16, s_pool[l], None,
                           row0=0, nseq=1, nb=T // bp, bm=bp, pos0=0)
        po, hist_s = _pool(rest, pcol, sample_hist[l], w_pool16, s_pool[l], po,
                           row0=T, nseq=SB, nb=1, bm=ST, pos0=P)

        y = _matmul_cat([fo, go, po], w_out[l].astype(BF16), name="proj_out")
        h, c = _norm_mid(h, y, g_post_mix[l], g_pre_mlp[l])
        (hid,) = _matmul(c, w_up[l].astype(BF16), [BF16], name="mlp_up", relu2=True)
        f = _matmul_kacc(hid, w_down[l].astype(BF16), name="mlp_down")
        if l + 1 < L:
            h, a = _norm_mid(h, f, g_post_mlp[l], g_pre_mix[l + 1])
        else:
            h = _norm_last(h, f, g_post_mlp[l])

        outs_p[0].append(k32[:T].reshape(1, T, H, FOX_HD))
        outs_p[1].append(v32[:T].reshape(1, T, H, FOX_HD))
        outs_p[2].append(logf[:T, :H].reshape(1, T, H))
        outs_p[3].append(gla_p)
        outs_p[4].append(hist_p[:, POOL_CARRY - POOL_HIST:])
        outs_s[0].append(k32[T:].reshape(SB, ST, H, FOX_HD))
        outs_s[1].append(v32[T:].reshape(SB, ST, H, FOX_HD))
        outs_s[2].append(logf[T:, :H].reshape(SB, ST, H))
        outs_s[3].append(gla_s)
        outs_s[4].append(hist_s[:, POOL_CARRY - POOL_HIST:])

    return (h[:T].reshape(1, T, D), h[T:].reshape(SB, ST, D),
            *[jnp.stack(o) for o in outs_p], *[jnp.stack(o) for o in outs_s])
```

```python
import functools
import math

import jax
import jax.numpy as jnp
from jax import lax
from jax.experimental import pallas as pl
from jax.experimental.pallas import tpu as pltpu

F32 = jnp.float32
BF16 = jnp.bfloat16

RMS_EPS = 1e-6
FOX_HD = 128
GLA_HEADS = 4
GLA_TAU = 16.0
GLA_CHUNK = 64
GLA_SUB_LOG2 = 4
GLA_SUB = 1 << GLA_SUB_LOG2
POOL_WINDOWS = (2, 4, 8, 16)
POOL_HIST = 15
POOL_CARRY = 16
LANES = 128
SUBLANES = 8
ONES_ROWS = 16
LOG2E = 1.4426950408889634
VMEM_LIMIT = 56 * 1024 * 1024


def _params(*sem):
    return pltpu.CompilerParams(dimension_semantics=sem, vmem_limit_bytes=VMEM_LIMIT)


def _pick(n, pref, mult=8):
    b = min(pref, n)
    b -= b % mult
    while b > 0 and n % b:
        b -= mult
    assert b > 0, (n, pref, mult)
    return b


def _rms(x):
    return x * lax.rsqrt(jnp.mean(x * x, axis=-1, keepdims=True) + RMS_EPS)


def _log_sigmoid(x):
    return jnp.minimum(x, 0.0) - jnp.log1p(jnp.exp(-jnp.abs(x)))


def _norm_first_body(h_ref, g_ref, a_ref):
    a_ref[...] = (_rms(h_ref[...]) * g_ref[...]).astype(a_ref.dtype)


def _norm_mid_body(h_ref, y_ref, gy_ref, g_ref, h_out, a_out):
    h = h_ref[...] + _rms(y_ref[...]) * gy_ref[...]
    h_out[...] = h
    a_out[...] = (_rms(h) * g_ref[...]).astype(a_out.dtype)


def _norm_last_body(h_ref, y_ref, gy_ref, h_out):
    h_out[...] = h_ref[...] + _rms(y_ref[...]) * gy_ref[...]


def _norm_first(h, g):
    M, D = h.shape
    bm = _pick(M, 256)
    row = pl.BlockSpec((bm, D), lambda i: (i, 0))
    vec = pl.BlockSpec((1, D), lambda i: (0, 0))
    return pl.pallas_call(
        _norm_first_body, grid=(M // bm,), in_specs=[row, vec], out_specs=row,
        out_shape=jax.ShapeDtypeStruct((M, D), BF16),
        compiler_params=_params("parallel"), name="norm_first")(h, g.reshape(1, D))


def _norm_mid(h, y, gy, g):
    M, D = h.shape
    bm = _pick(M, 256)
    row = pl.BlockSpec((bm, D), lambda i: (i, 0))
    vec = pl.BlockSpec((1, D), lambda i: (0, 0))
    return pl.pallas_call(
        _norm_mid_body, grid=(M // bm,), in_specs=[row, row, vec, vec], out_specs=[row, row],
        out_shape=[jax.ShapeDtypeStruct((M, D), F32), jax.ShapeDtypeStruct((M, D), BF16)],
        compiler_params=_params("parallel"), name="norm_mid")(h, y, gy.reshape(1, D), g.reshape(1, D))


def _norm_last(h, y, gy):
    M, D = h.shape
    bm = _pick(M, 256)
    row = pl.BlockSpec((bm, D), lambda i: (i, 0))
    vec = pl.BlockSpec((1, D), lambda i: (0, 0))
    return pl.pallas_call(
        _norm_last_body, grid=(M // bm,), in_specs=[row, row, vec], out_specs=row,
        out_shape=jax.ShapeDtypeStruct((M, D), F32),
        compiler_params=_params("parallel"), name="norm_last")(h, y, gy.reshape(1, D))


def _mm_body(x_ref, w_ref, *o_refs, relu2, scale):
    acc = jnp.dot(x_ref[...], w_ref[...], preferred_element_type=F32)
    if relu2:
        acc = jnp.square(jnp.maximum(acc, 0.0))
    if scale is not None:
        acc = acc * scale
    for o in o_refs:
        o[...] = acc.astype(o.dtype)


def _matmul(x, w, out_dtypes, *, name, relu2=False, scale=None, col0=0, ncols=None, bm=1024, bn=1024):
    M, K = x.shape
    N = w.shape[1] - col0 if ncols is None else ncols
    bm = _pick(M, bm)
    bn = _pick(N, bn, LANES)
    assert col0 % bn == 0
    c0 = col0 // bn
    outs = pl.pallas_call(
        functools.partial(_mm_body, relu2=relu2, scale=scale), grid=(M // bm, N // bn),
        in_specs=[pl.BlockSpec((bm, K), lambda i, j: (i, 0)),
                  pl.BlockSpec((K, bn), lambda i, j: (0, c0 + j))],
        out_specs=[pl.BlockSpec((bm, bn), lambda i, j: (i, j)) for _ in out_dtypes],
        out_shape=[jax.ShapeDtypeStruct((M, N), dt) for dt in out_dtypes],
        compiler_params=_params("parallel", "arbitrary"), name=name)(x, w)
    return outs


def _mm_heads_body(x_ref, w_ref, *refs, hd):
    o32_ref, o16_ref = refs[-2:]
    acc = jnp.dot(x_ref[...], w_ref[...], preferred_element_type=F32)
    o16_ref[...] = acc.astype(o16_ref.dtype)
    for h in range(acc.shape[1] // hd):
        o32_ref[:, h, :] = acc[:, h * hd:(h + 1) * hd]


def _matmul_heads(x, w, prev32, prev16, *, name, col0, ncols, row0, nrows, layer, nlayers, hd, bm=512, bn=1024):
    M, K = x.shape
    bm = _pick(math.gcd(row0, nrows) if row0 else nrows, bm)
    bn = _pick(ncols, bn, LANES * SUBLANES)
    assert col0 % bn == 0 and bn % hd == 0
    c0, r0, hb = col0 // bn, row0 // bm, bn // hd
    in_specs = [pl.BlockSpec((bm, K), lambda i, j: (r0 + i, 0)),
                pl.BlockSpec((K, bn), lambda i, j: (0, c0 + j))]
    args = [x, w]
    aliases = {}
    for out_idx, prev in enumerate((prev32, prev16)):
        if prev is not None:
            aliases[len(args)] = out_idx
            in_specs.append(pl.BlockSpec(memory_space=pl.ANY))
            args.append(prev)
    return pl.pallas_call(
        functools.partial(_mm_heads_body, hd=hd), grid=(nrows // bm, ncols // bn), in_specs=in_specs,
        out_specs=[pl.BlockSpec((None, bm, hb, hd), lambda i, j: (layer, i, j, 0)),
                   pl.BlockSpec((bm, bn), lambda i, j: (r0 + i, j))],
        out_shape=[jax.ShapeDtypeStruct((nlayers, nrows, ncols // hd, hd), F32),
                   jax.ShapeDtypeStruct((M, ncols), BF16)],
        input_output_aliases=aliases,
        compiler_params=_params("parallel", "arbitrary"), name=name)(*args)


def _mm_cat_body(*refs, splits):
    x_refs, w_ref, o_ref = refs[:-2], refs[-2], refs[-1]
    acc = None
    off = 0
    for x_ref, kk in zip(x_refs, splits):
        part = jnp.dot(x_ref[...], w_ref[off:off + kk, :], preferred_element_type=F32)
        acc = part if acc is None else acc + part
        off += kk
    o_ref[...] = acc


def _matmul_cat(xs, w, *, name, bm=1024, bn=1024):
    M = xs[0].shape[0]
    K, N = w.shape
    splits = tuple(x.shape[1] for x in xs)
    assert sum(splits) == K
    bm = _pick(M, bm)
    bn = _pick(N, bn, LANES)
    in_specs = [pl.BlockSpec((bm, kk), lambda i, j: (i, 0)) for kk in splits]
    in_specs.append(pl.BlockSpec((K, bn), lambda i, j: (0, j)))
    return pl.pallas_call(
        functools.partial(_mm_cat_body, splits=splits), grid=(M // bm, N // bn),
        in_specs=in_specs, out_specs=pl.BlockSpec((bm, bn), lambda i, j: (i, j)),
        out_shape=jax.ShapeDtypeStruct((M, N), F32),
        compiler_params=_params("parallel", "arbitrary"), name=name)(*xs, w)


def _mm_kacc_body(x_ref, w_ref, o_ref, *, bn):
    k = pl.program_id(1)

    @pl.when(k == 0)
    def _():
        o_ref[...] = jnp.zeros(o_ref.shape, F32)

    x = x_ref[...]
    for n0 in range(0, o_ref.shape[1], bn):
        o_ref[:, n0:n0 + bn] += jnp.dot(x, w_ref[:, n0:n0 + bn], preferred_element_type=F32)


def _matmul_kacc(x, w, *, name, bm=512, bk=1024):
    M, K = x.shape
    N = w.shape[1]
    bm = _pick(M, bm)
    bk = _pick(K, bk, LANES)
    return pl.pallas_call(
        functools.partial(_mm_kacc_body, bn=_pick(N, 512, LANES)), grid=(M // bm, K // bk),
        in_specs=[pl.BlockSpec((bm, bk), lambda i, k: (i, k)),
                  pl.BlockSpec((bk, N), lambda i, k: (k, 0))],
        out_specs=pl.BlockSpec((bm, N), lambda i, k: (i, 0)),
        out_shape=jax.ShapeDtypeStruct((M, N), F32),
        compiler_params=_params("parallel", "arbitrary"), name=name)(x, w)


def _gates_body(t_ref, bf_ref, w2_ref, bg_ref, logf_ref, la_ref):
    t = t_ref[...]
    logf_ref[...] = _log_sigmoid(t + bf_ref[...])
    zg = jnp.dot(t, w2_ref[...], preferred_element_type=F32) + bg_ref[...]
    la_ref[...] = _log_sigmoid(zg) * (1.0 / GLA_TAU)


def _gates(tail, bfox_pad, w2_pad, bgla):
    M = tail.shape[0]
    KW = w2_pad.shape[1]
    bm = _pick(M, 512)
    return pl.pallas_call(
        _gates_body, grid=(M // bm,),
        in_specs=[pl.BlockSpec((bm, LANES), lambda i: (i, 0)),
                  pl.BlockSpec((1, LANES), lambda i: (0, 0)),
                  pl.BlockSpec((LANES, KW), lambda i: (0, 0)),
                  pl.BlockSpec((1, KW), lambda i: (0, 0))],
        out_specs=[pl.BlockSpec((bm, LANES), lambda i: (i, 0)),
                   pl.BlockSpec((bm, KW), lambda i: (i, 0))],
        out_shape=[jax.ShapeDtypeStruct((M, LANES), F32), jax.ShapeDtypeStruct((M, KW), F32)],
        compiler_params=_params("parallel"), name="gates")(tail, bfox_pad, w2_pad, bgla)


def _row_cumsum(x):
    n = x.shape[0]
    row = lax.broadcasted_iota(jnp.int32, x.shape, 0)
    sh = 1
    while sh < n:
        x = x + jnp.where(row >= sh, pltpu.roll(x, sh, axis=0), 0.0)
        sh *= 2
    return x


def _cumsum_body(x_ref, c_ref, carry_sc):
    @pl.when(pl.program_id(1) == 0)
    def _():
        carry_sc[...] = jnp.zeros(carry_sc.shape, F32)

    c = _row_cumsum(x_ref[...]) + carry_sc[...]
    c_ref[...] = c
    n = c.shape[0]
    carry_sc[...] = c[n - 1:n, :]


def _cumsum(x, bl):
    S, L, W = x.shape
    bl = _pick(L, bl)
    spec = pl.BlockSpec((None, bl, W), lambda s, i: (s, i, 0))
    return pl.pallas_call(
        _cumsum_body, grid=(S, L // bl), in_specs=[spec], out_specs=spec,
        out_shape=jax.ShapeDtypeStruct((S, L, W), F32),
        scratch_shapes=[pltpu.VMEM((1, W), F32)],
        compiler_params=_params("parallel", "arbitrary"), name="cumsum")(x)


_NT = (((1,), (1,)), ((), ()))
_TN = (((0,), (0,)), ((), ()))


def _fox_prompt_body(q_ref, k_ref, vt_ref, b_ref, o_ref, acc_sc, st_sc, *, tq, tk, nsub):
    qi = pl.program_id(1)
    hd = k_ref.shape[1]
    bq = nsub * tq
    nkb = bq // tk
    ones = jnp.ones((ONES_ROWS, tk), BF16)
    rep = tq // LANES

    def scores(j, sub, mask_off):
        s0 = pl.multiple_of(j * tk, tk)
        q = q_ref[sub * tq:(sub + 1) * tq, :]
        k = k_ref[pl.ds(s0, tk), :]
        st = lax.dot_general(k, q, _NT, preferred_element_type=F32)
        b = b_ref[pl.ds(s0, tk), :]
        st = st + jnp.concatenate([b] * rep, axis=1)
        if mask_off is not None:
            key = lax.broadcasted_iota(jnp.int32, (tk, tq), 0) + mask_off
            qry = lax.broadcasted_iota(jnp.int32, (tk, tq), 1)
            st = jnp.where(key <= qry, st, -jnp.inf)
        return st

    def mask(st, mask_off):
        key = lax.broadcasted_iota(jnp.int32, (tk, tq), 0) + mask_off
        qry = lax.broadcasted_iota(jnp.int32, (tk, tq), 1)
        return jnp.where(key <= qry, st, -jnp.inf)

    def scores(j, sub):
        s0 = pl.multiple_of(j * tk, tk)
        q = q_ref[sub * tq:(sub + 1) * tq, :]
        k = k_ref[pl.ds(s0, tk), :]
        st = lax.dot_general(k, q, _NT, preferred_element_type=F32)
        b = b_ref[pl.ds(s0, tk), :]
        return st + jnp.concatenate([b] * rep, axis=1)

    def accumulate(j, sub, st, m_blk, m_prev):
        s0 = pl.multiple_of(j * tk, tk)
        m_new = jnp.maximum(m_prev, m_blk)
        alpha = jnp.exp2(m_prev - m_new)
        pt = jnp.exp2(st - m_new).astype(BF16)
        vt = jnp.concatenate([vt_ref[:, pl.ds(s0, tk)], ones], axis=0)
        acc_sc[sub] = alpha * acc_sc[sub] + jnp.dot(vt, pt, preferred_element_type=F32)
        return m_new

    def issue(j, slot):
        mb = []
        for sub in range(nsub):
            st = scores(j, sub)
            st_sc[slot, sub] = st
            mb.append(jnp.max(st, axis=0, keepdims=True))
        return tuple(mb)

    acc_sc[...] = jnp.zeros(acc_sc.shape, F32)
    m0 = tuple(jnp.full((1, tq), -jnp.inf, F32) for _ in range(nsub))
    n = nkb * qi

    def body(jj, carry):
        ms, mb = carry
        for slot in range(2):
            j = 2 * jj + slot
            mb_next = issue(j + 1, 1 - slot)
            ms = tuple(accumulate(j, sub, st_sc[slot, sub], mb[sub], ms[sub]) for sub in range(nsub))
            mb = mb_next
        return ms, mb

    assert nkb % 2 == 0
    ms, _ = lax.fori_loop(0, n // 2, body, (m0, issue(0, 0)))
    ms = list(ms)
    for d in range(nkb):
        for sub in range(nsub):
            k_lo, k_hi = d * tk, (d + 1) * tk - 1
            q_lo, q_hi = sub * tq, (sub + 1) * tq - 1
            if k_lo > q_hi:
                continue
            st = st_sc[0, sub] if d == 0 else scores(n + d, sub)
            if k_hi > q_lo:
                st = mask(st, k_lo - q_lo)
            ms[sub] = accumulate(n + d, sub, st, jnp.max(st, axis=0, keepdims=True), ms[sub])
    for sub in range(nsub):
        acc = acc_sc[sub]
        o = acc[:hd, :] / acc[hd:hd + 1, :]
        o_ref[sub * tq:(sub + 1) * tq, :] = jnp.transpose(o).astype(o_ref.dtype)


FOX_TQ, FOX_TK, FOX_NSUB = 512, 512, 2


def _fox_prompt(q16, k16, vt16, brep, T):
    M, W = q16.shape
    H = W // FOX_HD
    tq, tk, nsub = FOX_TQ, FOX_TK, FOX_NSUB
    bq = nsub * tq
    assert T % bq == 0 and bq % tk == 0
    return pl.pallas_call(
        functools.partial(_fox_prompt_body, tq=tq, tk=tk, nsub=nsub), grid=(H, T // bq),
        in_specs=[pl.BlockSpec((bq, FOX_HD), lambda h, i: (i, h)),
                  pl.BlockSpec((T, FOX_HD), lambda h, i: (0, h)),
                  pl.BlockSpec((FOX_HD, T), lambda h, i: (h, 0)),
                  pl.BlockSpec((None, T, LANES), lambda h, i: (h, 0, 0))],
        out_specs=pl.BlockSpec((bq, FOX_HD), lambda h, i: (i, h)),
        out_shape=jax.ShapeDtypeStruct((M, W), BF16),
        scratch_shapes=[pltpu.VMEM((nsub, FOX_HD + ONES_ROWS, tq), F32),
                        pltpu.VMEM((2, nsub, tk, tq), F32)],
        compiler_params=_params("parallel", "arbitrary"), name="fox_prompt")(q16, k16, vt16, brep)


def _fox_sample_body(q_ref, kn_ref, vn_ref, kp_ref, vp_ref, c_ref, fo_any, o_ref, *, P, ST, HB):
    del fo_any
    r = lax.broadcasted_iota(jnp.int32, (ST, ST), 0)
    cc = lax.broadcasted_iota(jnp.int32, (ST, ST), 1)
    hd = FOX_HD
    scores = []
    for h in range(HB):
        sl = slice(h * hd, (h + 1) * hd)
        q = q_ref[:, sl]
        c = c_ref[h]
        cb = c[:, P - 1:P]
        sp = (lax.dot_general(q, kp_ref[:, sl].astype(BF16), _NT, preferred_element_type=F32)
              + (cb - c[:, :P]) * LOG2E)
        sn = (lax.dot_general(q, kn_ref[:, sl], _NT, preferred_element_type=F32)
              + (cb - c[:, P:]) * LOG2E)
        scores.append((sp, jnp.where(cc <= r, sn, -jnp.inf)))
    for h, (sp, sn) in enumerate(scores):
        sl = slice(h * hd, (h + 1) * hd)
        m = jnp.maximum(jnp.max(sp, axis=1, keepdims=True), jnp.max(sn, axis=1, keepdims=True))
        pp = jnp.exp2(sp - m)
        pn = jnp.exp2(sn - m)
        l = jnp.sum(pp, axis=1, keepdims=True) + jnp.sum(pn, axis=1, keepdims=True)
        o = (jnp.dot(pp.astype(BF16), vp_ref[:, sl].astype(BF16), preferred_element_type=F32)
             + jnp.dot(pn.astype(BF16), vn_ref[:, sl], preferred_element_type=F32))
        o_ref[:, sl] = (o / l).astype(o_ref.dtype)


def _fox_sample(fo, q16, k16, v16, cache_k, cache_v, ct, layer, T, SB, ST):
    M, W = q16.shape
    H = W // FOX_HD
    P = cache_k.shape[2]
    r0 = T // ST
    HB = 4 if H % 4 == 0 else 1
    new = pl.BlockSpec((ST, HB * FOX_HD), lambda b, h: (r0 + b, h))
    past = pl.BlockSpec((None, None, P, HB * FOX_HD), lambda b, h: (layer, b, 0, h))
    return pl.pallas_call(
        functools.partial(_fox_sample_body, P=P, ST=ST, HB=HB), grid=(SB, H // HB),
        in_specs=[new, new, new, past, past,
                  pl.BlockSpec((None, HB, 1, P + ST), lambda b, h: (b, h, 0, 0)),
                  pl.BlockSpec(memory_space=pl.ANY)],
        out_specs=new,
        out_shape=jax.ShapeDtypeStruct((M, W), BF16),
        input_output_aliases={6: 0},
        compiler_params=_params("parallel", "arbitrary"), name="fox_sample")(
            q16, k16, v16, cache_k, cache_v, ct, fo)


def _gla_body(q_ref, k_ref, v_ref, r_ref, la_ref, s0_ref, g_ref, go_any, o_ref, sout_ref,
              S_sc, b_sc, q_sc, *, C, NH, DK, DV, nc):
    del go_any
    c = pl.program_id(1)

    @pl.when(c == 0)
    def _():
        S_sc[...] = s0_ref[...]

    b = _row_cumsum(la_ref[...])
    b_sc[...] = b
    q = q_ref[...] * (DK ** -0.5)
    q_sc[...] = q
    bl = b[C - 1:C, :]
    qe = (q * jnp.exp(b)).astype(BF16)
    kdec = (k_ref[...] * jnp.exp(bl - b)).astype(BF16)
    ebl = jnp.exp(bl)
    v16 = v_ref[...].astype(BF16)

    R = GLA_SUB
    nb = C // R
    k = k_ref[...]
    srow_c = lax.broadcasted_iota(jnp.int32, b.shape, 0)
    at_off = [None] * NH
    for i in range(1, nb):
        b_ref_i = b[R * i - 1:R * i, :]
        in_blk = (srow_c >> GLA_SUB_LOG2) == i
        qt = (q * jnp.exp(jnp.where(in_blk, b - b_ref_i, -jnp.inf))).astype(BF16)
        kt = (k * jnp.exp(jnp.where(srow_c < R * i, b_ref_i - b, -jnp.inf))).astype(BF16)
        for h in range(NH):
            sl = slice(h * DK, (h + 1) * DK)
            part = lax.dot_general(kt[:, sl], qt[:, sl], _NT, preferred_element_type=F32)
            at_off[h] = part if at_off[h] is None else at_off[h] + part

    srow = lax.broadcasted_iota(jnp.int32, (R, DK), 0)
    lane = lax.broadcasted_iota(jnp.int32, (R, C), 1)
    at_diag = [[] for _ in range(NH)]
    for i in range(nb):
        blk = slice(R * i, R * (i + 1))
        ats = [jnp.zeros((R, C), F32) for _ in range(NH)]
        for r in range(R):
            t = R * i + r
            mask = srow <= r
            lm = lane == t
            for h in range(NH):
                sl = slice(h * DK, (h + 1) * DK)
                e = jnp.exp(jnp.where(mask, b_sc[t:t + 1, sl] - b_sc[blk, sl], -jnp.inf))
                w = e * (q_sc[t:t + 1, sl] * k_ref[blk, sl])
                ats[h] = jnp.where(lm, jnp.sum(w, axis=1, keepdims=True), ats[h])
        for h in range(NH):
            at_diag[h].append(ats[h])
    ats = []
    for h in range(NH):
        at = jnp.concatenate(at_diag[h], axis=0)
        ats.append(at if at_off[h] is None else at + at_off[h])

    g = g_ref[...]
    for h in range(NH):
        sl = slice(h * DK, (h + 1) * DK)
        vs = slice(h * DV, (h + 1) * DV)
        S = S_sc[h]
        o = (jnp.dot(qe[:, sl], S.astype(BF16), preferred_element_type=F32)
             + lax.dot_general(ats[h].astype(BF16), v16[:, vs], _TN, preferred_element_type=F32))
        r = r_ref[:, vs]
        o = _rms(o) * g * (r * (1.0 / (1.0 + jnp.exp(-r))))
        o_ref[:, vs] = o.astype(o_ref.dtype)
        dm = jnp.transpose(jnp.broadcast_to(ebl[:, sl], (DK, DK)))
        decay = jnp.concatenate([dm] * (DV // DK), axis=1)
        S_sc[h] = decay * S + lax.dot_general(kdec[:, sl], v16[:, vs], _TN, preferred_element_type=F32)

    @pl.when(c == nc - 1)
    def _():
        sout_ref[...] = S_sc[...]


def _gla(rest, la, s0, g, go, *, row0, nseq, nc, C, DK, DV):
    M = rest.shape[0]
    NH = GLA_HEADS
    KW, VW = NH * DK, NH * DV
    r0 = row0 // C
    assert row0 % C == 0 and VW == 2 * KW

    def rows(col):
        return lambda s, c: (r0 + s * nc + c, col)

    in_specs = [pl.BlockSpec((C, KW), rows(0)), pl.BlockSpec((C, KW), rows(1)),
                pl.BlockSpec((C, VW), rows(1)), pl.BlockSpec((C, VW), rows(2)),
                pl.BlockSpec((C, KW), rows(0)),
                pl.BlockSpec((None, NH, DK, DV), lambda s, c: (s, 0, 0, 0)),
                pl.BlockSpec((1, DV), lambda s, c: (0, 0))]
    args = [rest, rest, rest, rest, la, s0, g.reshape(1, DV)]
    aliases = {}
    if go is not None:
        in_specs.append(pl.BlockSpec(memory_space=pl.ANY))
        args.append(go)
        aliases = {7: 0}
        body = _gla_body
    else:
        body = functools.partial(_gla_body_noalias)
    return pl.pallas_call(
        functools.partial(body, C=C, NH=NH, DK=DK, DV=DV, nc=nc), grid=(nseq, nc),
        in_specs=in_specs,
        out_specs=[pl.BlockSpec((C, VW), rows(0)),
                   pl.BlockSpec((None, NH, DK, DV), lambda s, c: (s, 0, 0, 0))],
        out_shape=[jax.ShapeDtypeStruct((M, VW), BF16),
                   jax.ShapeDtypeStruct((nseq, NH, DK, DV), F32)],
        scratch_shapes=[pltpu.VMEM((NH, DK, DV), F32), pltpu.VMEM((C, KW), F32),
                        pltpu.VMEM((C, KW), F32)],
        input_output_aliases=aliases,
        compiler_params=_params("parallel", "arbitrary"), name="gla")(*args)


def _gla_body_noalias(q_ref, k_ref, v_ref, r_ref, la_ref, s0_ref, g_ref, o_ref, sout_ref,
                      S_sc, b_sc, q_sc, **kw):
    _gla_body(q_ref, k_ref, v_ref, r_ref, la_ref, s0_ref, g_ref, None, o_ref, sout_ref,
              S_sc, b_sc, q_sc, **kw)


def _pool_body(*refs, bm, pos0, GC, aliased):
    if aliased:
        p_ref, hist_ref, w_ref, sp_ref, _, o_ref, hout_ref, xa_sc = refs
    else:
        p_ref, hist_ref, w_ref, sp_ref, o_ref, hout_ref, xa_sc = refs
    i = pl.program_id(1)

    @pl.when(i == 0)
    def _():
        xa_sc[0:POOL_CARRY, :] = hist_ref[...]

    xa_sc[POOL_CARRY:POOL_CARRY + bm, :] = p_ref[...]
    pos = pos0 + i * bm + lax.broadcasted_iota(jnp.int32, (bm, 1), 0)
    for gi, w in enumerate(POOL_WINDOWS):
        cs = slice(gi * GC, (gi + 1) * GC)
        x = xa_sc[POOL_CARRY:POOL_CARRY + bm, cs]
        tot = x
        for d in range(1, w):
            tot = tot + xa_sc[POOL_CARRY - d:POOL_CARRY - d + bm, cs]
        cnt = jnp.minimum(pos + 1, w).astype(F32)
        dd = (tot / cnt - x).astype(BF16)
        y = jnp.dot(dd, w_ref[gi], preferred_element_type=F32) * sp_ref[:, cs]
        o_ref[:, cs] = y.astype(o_ref.dtype)
    last = xa_sc[bm:bm + POOL_CARRY, :]
    hout_ref[...] = last
    xa_sc[0:POOL_CARRY, :] = last


def _pool(rest, col, hist, w16, sp, po, *, row0, nseq, nb, bm, pos0):
    M = rest.shape[0]
    G, GC, _ = w16.shape
    PW = G * GC
    r0 = row0 // bm
    assert row0 % bm == 0 and bm >= POOL_CARRY
    in_specs = [pl.BlockSpec((bm, PW), lambda s, i: (r0 + s * nb + i, col)),
                pl.BlockSpec((None, POOL_CARRY, PW), lambda s, i: (s, 0, 0)),
                pl.BlockSpec((G, GC, GC), lambda s, i: (0, 0, 0)),
                pl.BlockSpec((1, PW), lambda s, i: (0, 0))]
    args = [rest, hist, w16, sp.reshape(1, PW)]
    aliases = {}
    if po is not None:
        in_specs.append(pl.BlockSpec(memory_space=pl.ANY))
        args.append(po)
        aliases = {4: 0}
    return pl.pallas_call(
        functools.partial(_pool_body, bm=bm, pos0=pos0, GC=GC, aliased=po is not None),
        grid=(nseq, nb), in_specs=in_specs,
        out_specs=[pl.BlockSpec((bm, PW), lambda s, i: (r0 + s * nb + i, 0)),
                   pl.BlockSpec((None, POOL_CARRY, PW), lambda s, i: (s, 0, 0))],
        out_shape=[jax.ShapeDtypeStruct((M, PW), BF16),
                   jax.ShapeDtypeStruct((nseq, POOL_CARRY, PW), F32)],
        scratch_shapes=[pltpu.VMEM((POOL_CARRY + bm, PW), F32)],
        input_output_aliases=aliases,
        compiler_params=_params("parallel", "arbitrary"), name="pool")(*args)


def kernel(x_prompt, x_sample, cache_fox_k, cache_fox_v, cache_fox_logf, state_gla, state_pool, g_pre_mix, w_in, b_fox_f, w_gla_a2, b_gla_a, g_gla_norm, w_pool, s_pool, w_out, g_post_mix, g_pre_mlp, w_up, w_down, g_post_mlp):
    B, T, D = x_prompt.shape
    SB, ST, _ = x_sample.shape
    L = w_in.shape[0]
    P = cache_fox_k.shape[2]
    assert B == 1
    FW = D // 2
    H = FW // FOX_HD
    GW = D // 4
    DV = GW // GLA_HEADS
    DK = DV // 2
    KW = GLA_HEADS * DK
    RANK = w_gla_a2.shape[1]
    PW = D - FW - GW
    GC = PW // len(POOL_WINDOWS)
    MS = SB * ST
    M = T + MS
    assert H + RANK <= LANES and 2 * KW + 2 * GW == 3 * PW and PW % LANES == 0

    o_fq, o_fk, o_fv, o_ff = 0, FW, 2 * FW, 3 * FW
    o_gq = o_ff + H
    o_ga = o_gq + 2 * KW + GW
    o_gr = o_ga + RANK

    cache_k = cache_fox_k.reshape(L, SB, P, FW)
    cache_v = cache_fox_v.reshape(L, SB, P, FW)
    past_logf = jnp.pad(cache_fox_logf.astype(F32), ((0, 0), (0, 0), (0, 0), (0, LANES - H)))
    zero_state = jnp.zeros((1, GLA_HEADS, DK, DV), F32)
    zero_hist = jnp.zeros((1, POOL_CARRY, PW), F32)
    sample_hist = jnp.pad(state_pool, ((0, 0), (0, 0), (POOL_CARRY - POOL_HIST, 0), (0, 0)))

    h = jnp.concatenate([x_prompt.reshape(T, D), x_sample.reshape(MS, D)], axis=0)
    a = _norm_first(h, g_pre_mix[0])
    outs_p = [[] for _ in range(5)]
    outs_s = [[] for _ in range(5)]
    kv_p, kv_s = [None, None], [None, None]
    for l in range(L):
        wi = w_in[l].astype(BF16)
        w_rest = jnp.concatenate([wi[:, o_gq:o_ga], wi[:, o_gr:]], axis=1)
        w_tail = jnp.concatenate([wi[:, o_ff:o_gq], wi[:, o_ga:o_gr],
                                  jnp.zeros((D, LANES - H - RANK), BF16)], axis=1)
        bfox_pad = jnp.pad(b_fox_f[l].astype(F32), (0, LANES - H)).reshape(1, LANES)
        w2_pad = jnp.pad(w_gla_a2[l].astype(F32), ((H, LANES - H - RANK), (0, 0)))

        (q16,) = _matmul(a, wi, [BF16], name="proj_q", col0=o_fq, ncols=FW,
                         scale=FOX_HD ** -0.5 * LOG2E)
        kv16 = []
        for t, (name, col0) in enumerate((("proj_k", o_fk), ("proj_v", o_fv))):
            common = dict(name=name, col0=col0, ncols=FW, layer=l, nlayers=L, hd=FOX_HD)
            kv_p[t], x16 = _matmul_heads(a, wi, kv_p[t], None, row0=0, nrows=T, **common)
            kv_s[t], x16 = _matmul_heads(a, wi, kv_s[t], x16, row0=T, nrows=MS, **common)
            kv16.append(x16)
        k16, v16 = kv16
        (rest,) = _matmul(a, w_rest, [F32], name="proj_rest")
        (tail,) = _matmul(a, w_tail, [F32], name="proj_tail")
        logf, la = _gates(tail, bfox_pad, w2_pad, b_gla_a[l].reshape(1, KW))

        c_p = _cumsum(logf[:T].reshape(1, T, LANES), 1024)
        brep = jnp.broadcast_to(jnp.transpose(c_p[0, :, :H] * (-LOG2E))[:, :, None], (H, T, LANES))
        vt16 = jnp.transpose(v16[:T])
        fo = _fox_prompt(q16, k16, vt16, brep, T)
        lf_all = jnp.concatenate([past_logf[l], logf[T:].reshape(SB, ST, LANES)], axis=1)
        c_s = _cumsum(lf_all, P + ST)
        ct_s = jnp.transpose(c_s[:, :, :H], (0, 2, 1)).reshape(SB, H, 1, P + ST)
        fo = _fox_sample(fo, q16, k16, v16, cache_k, cache_v, ct_s, l, T, SB, ST)

        go, gla_p = _gla(rest, la, zero_state, g_gla_norm[l], None,
                         row0=0, nseq=1, nc=T // GLA_CHUNK, C=GLA_CHUNK, DK=DK, DV=DV)
        go, gla_s = _gla(rest, la, state_gla[l], g_gla_norm[l], go,
                         row0=T, nseq=SB, nc=1, C=ST, DK=DK, DV=DV)

        w_pool16 = w_pool[l].astype(BF16)
        pcol = (2 * KW + 2 * GW) // PW
        bp = _pick(T, 512)
        po, hist_p = _pool(rest, pcol, zero_hist, w_pool16, s_pool[l], None,
                           row0=0, nseq=1, nb=T // bp, bm=bp, pos0=0)
        po, hist_s = _pool(rest, pcol, sample_hist[l], w_pool16, s_pool[l], po,
                           row0=T, nseq=SB, nb=1, bm=ST, pos0=P)

        y = _matmul_cat([fo, go, po], w_out[l].astype(BF16), name="proj_out")
        h, c = _norm_mid(h, y, g_post_mix[l], g_pre_mlp[l])
        (hid,) = _matmul(c, w_up[l].astype(BF16), [BF16], name="mlp_up", relu2=True)
        f = _matmul_kacc(hid, w_down[l].astype(BF16), name="mlp_down")
        if l + 1 < L:
            h, a = _norm_mid(h, f, g_post_mlp[l], g_pre_mix[l + 1])
        else:
            h = _norm_last(h, f, g_post_mlp[l])

        outs_p[2].append(logf[:T, :H].reshape(1, T, H))
        outs_p[3].append(gla_p)
        outs_p[4].append(hist_p[:, POOL_CARRY - POOL_HIST:])
        outs_s[2].append(logf[T:, :H].reshape(SB, ST, H))
        outs_s[3].append(gla_s)
        outs_s[4].append(hist_s[:, POOL_CARRY - POOL_HIST:])

    p_kv = [x.reshape(L, 1, T, H, FOX_HD) for x in kv_p]
    s_kv = [x.reshape(L, SB, ST, H, FOX_HD) for x in kv_s]
    return (h[:T].reshape(1, T, D), h[T:].reshape(SB, ST, D),
            *p_kv, *[jnp.stack(o) for o in outs_p[2:]], *s_kv, *[jnp.stack(o) for o in outs_s[2:]])
```

```python
import functools
import math

import jax
import jax.numpy as jnp
from jax import lax
from jax.experimental import pallas as pl
from jax.experimental.pallas import tpu as pltpu

F32 = jnp.float32
BF16 = jnp.bfloat16

RMS_EPS = 1e-6
FOX_HD = 128
GLA_HEADS = 4
GLA_TAU = 16.0
GLA_CHUNK = 64
GLA_SUB_LOG2 = 4
GLA_SUB = 1 << GLA_SUB_LOG2
POOL_WINDOWS = (2, 4, 8, 16)
POOL_HIST = 15
POOL_CARRY = 16
LANES = 128
SUBLANES = 8
ONES_ROWS = 16
LOG2E = 1.4426950408889634
VMEM_LIMIT = 56 * 1024 * 1024


def _params(*sem):
    return pltpu.CompilerParams(dimension_semantics=sem, vmem_limit_bytes=VMEM_LIMIT)


def _pick(n, pref, mult=8):
    b = min(pref, n)
    b -= b % mult
    while b > 0 and n % b:
        b -= mult
    assert b > 0, (n, pref, mult)
    return b


def _rms(x):
    return x * lax.rsqrt(jnp.mean(x * x, axis=-1, keepdims=True) + RMS_EPS)


def _log_sigmoid(x):
    return jnp.minimum(x, 0.0) - jnp.log1p(jnp.exp(-jnp.abs(x)))


def _norm_first_body(h_ref, g_ref, a_ref):
    a_ref[...] = (_rms(h_ref[...]) * g_ref[...]).astype(a_ref.dtype)


def _norm_mid_body(h_ref, y_ref, gy_ref, g_ref, h_out, a_out):
    h = h_ref[...] + _rms(y_ref[...]) * gy_ref[...]
    h_out[...] = h
    a_out[...] = (_rms(h) * g_ref[...]).astype(a_out.dtype)


def _norm_last_body(h_ref, y_ref, gy_ref, h_out):
    h_out[...] = h_ref[...] + _rms(y_ref[...]) * gy_ref[...]


def _norm_first(h, g):
    M, D = h.shape
    bm = _pick(M, 256)
    row = pl.BlockSpec((bm, D), lambda i: (i, 0))
    vec = pl.BlockSpec((1, D), lambda i: (0, 0))
    return pl.pallas_call(
        _norm_first_body, grid=(M // bm,), in_specs=[row, vec], out_specs=row,
        out_shape=jax.ShapeDtypeStruct((M, D), BF16),
        compiler_params=_params("parallel"), name="norm_first")(h, g.reshape(1, D))


def _norm_mid(h, y, gy, g):
    M, D = h.shape
    bm = _pick(M, 256)
    row = pl.BlockSpec((bm, D), lambda i: (i, 0))
    vec = pl.BlockSpec((1, D), lambda i: (0, 0))
    return pl.pallas_call(
        _norm_mid_body, grid=(M // bm,), in_specs=[row, row, vec, vec], out_specs=[row, row],
        out_shape=[jax.ShapeDtypeStruct((M, D), F32), jax.ShapeDtypeStruct((M, D), BF16)],
        compiler_params=_params("parallel"), name="norm_mid")(h, y, gy.reshape(1, D), g.reshape(1, D))


def _norm_last(h, y, gy, row0, nrows):
    D = h.shape[1]
    bm = _pick(math.gcd(row0, nrows) if row0 else nrows, 256)
    r0 = row0 // bm
    row = pl.BlockSpec((bm, D), lambda i: (r0 + i, 0))
    vec = pl.BlockSpec((1, D), lambda i: (0, 0))
    return pl.pallas_call(
        _norm_last_body, grid=(nrows // bm,), in_specs=[row, row, vec],
        out_specs=pl.BlockSpec((bm, D), lambda i: (i, 0)),
        out_shape=jax.ShapeDtypeStruct((nrows, D), F32),
        compiler_params=_params("parallel"), name="norm_last")(h, y, gy.reshape(1, D))


def _mm_body(x_ref, w_ref, *o_refs, relu2, scale):
    acc = jnp.dot(x_ref[...], w_ref[...], preferred_element_type=F32)
    if relu2:
        acc = jnp.square(jnp.maximum(acc, 0.0))
    if scale is not None:
        acc = acc * scale
    for o in o_refs:
        o[...] = acc.astype(o.dtype)


def _matmul(x, w, layer, out_dtypes, *, name, relu2=False, scale=None, col0=0, ncols=None, bm=1024, bn=1024):
    M, K = x.shape
    N = w.shape[2] - col0 if ncols is None else ncols
    bm = _pick(M, bm)
    bn = _pick(N, bn, LANES)
    assert col0 % bn == 0
    c0 = col0 // bn
    outs = pl.pallas_call(
        functools.partial(_mm_body, relu2=relu2, scale=scale), grid=(M // bm, N // bn),
        in_specs=[pl.BlockSpec((bm, K), lambda i, j: (i, 0)),
                  pl.BlockSpec((None, K, bn), lambda i, j: (layer, 0, c0 + j))],
        out_specs=[pl.BlockSpec((bm, bn), lambda i, j: (i, j)) for _ in out_dtypes],
        out_shape=[jax.ShapeDtypeStruct((M, N), dt) for dt in out_dtypes],
        compiler_params=_params("parallel", "arbitrary"), name=name)(x, w)
    return outs


def _mm_heads_body(x_ref, w_ref, *refs, hd):
    o32_ref, o16_ref = refs[-2:]
    acc = jnp.dot(x_ref[...], w_ref[...], preferred_element_type=F32)
    o16_ref[...] = acc.astype(o16_ref.dtype)
    for h in range(acc.shape[1] // hd):
        o32_ref[:, h, :] = acc[:, h * hd:(h + 1) * hd]


def _matmul_heads(x, w, prev32, prev16, *, name, col0, ncols, row0, nrows, layer, nlayers, hd, bm=512, bn=1024):
    M, K = x.shape
    bm = _pick(math.gcd(row0, nrows) if row0 else nrows, bm)
    bn = _pick(ncols, bn, LANES * SUBLANES)
    assert col0 % bn == 0 and bn % hd == 0
    c0, r0, hb = col0 // bn, row0 // bm, bn // hd
    in_specs = [pl.BlockSpec((bm, K), lambda i, j: (r0 + i, 0)),
                pl.BlockSpec((None, K, bn), lambda i, j: (layer, 0, c0 + j))]
    args = [x, w]
    aliases = {}
    for out_idx, prev in enumerate((prev32, prev16)):
        if prev is not None:
            aliases[len(args)] = out_idx
            in_specs.append(pl.BlockSpec(memory_space=pl.ANY))
            args.append(prev)
    return pl.pallas_call(
        functools.partial(_mm_heads_body, hd=hd), grid=(nrows // bm, ncols // bn), in_specs=in_specs,
        out_specs=[pl.BlockSpec((None, bm, hb, hd), lambda i, j: (layer, i, j, 0)),
                   pl.BlockSpec((bm, bn), lambda i, j: (r0 + i, j))],
        out_shape=[jax.ShapeDtypeStruct((nlayers, nrows, ncols // hd, hd), F32),
                   jax.ShapeDtypeStruct((M, ncols), BF16)],
        input_output_aliases=aliases,
        compiler_params=_params("parallel", "arbitrary"), name=name)(*args)


def _mm_cat_body(*refs, splits):
    x_refs, w_ref, o_ref = refs[:-2], refs[-2], refs[-1]
    acc = None
    off = 0
    for x_ref, kk in zip(x_refs, splits):
        part = jnp.dot(x_ref[...], w_ref[off:off + kk, :], preferred_element_type=F32)
        acc = part if acc is None else acc + part
        off += kk
    o_ref[...] = acc


def _matmul_cat(xs, w, layer, *, name, bm=1024, bn=1024):
    M = xs[0].shape[0]
    _, K, N = w.shape
    splits = tuple(x.shape[1] for x in xs)
    assert sum(splits) == K
    bm = _pick(M, bm)
    bn = _pick(N, bn, LANES)
    in_specs = [pl.BlockSpec((bm, kk), lambda i, j: (i, 0)) for kk in splits]
    in_specs.append(pl.BlockSpec((None, K, bn), lambda i, j: (layer, 0, j)))
    return pl.pallas_call(
        functools.partial(_mm_cat_body, splits=splits), grid=(M // bm, N // bn),
        in_specs=in_specs, out_specs=pl.BlockSpec((bm, bn), lambda i, j: (i, j)),
        out_shape=jax.ShapeDtypeStruct((M, N), F32),
        compiler_params=_params("parallel", "arbitrary"), name=name)(*xs, w)


def _mm_kacc_body(x_ref, w_ref, o_ref, *, bn):
    k = pl.program_id(1)

    @pl.when(k == 0)
    def _():
        o_ref[...] = jnp.zeros(o_ref.shape, F32)

    x = x_ref[...]
    for n0 in range(0, o_ref.shape[1], bn):
        o_ref[:, n0:n0 + bn] += jnp.dot(x, w_ref[:, n0:n0 + bn], preferred_element_type=F32)


def _matmul_kacc(x, w, layer, *, name, bm=512, bk=1024):
    M, K = x.shape
    N = w.shape[2]
    bm = _pick(M, bm)
    bk = _pick(K, bk, LANES)
    return pl.pallas_call(
        functools.partial(_mm_kacc_body, bn=_pick(N, 512, LANES)), grid=(M // bm, K // bk),
        in_specs=[pl.BlockSpec((bm, bk), lambda i, k: (i, k)),
                  pl.BlockSpec((None, bk, N), lambda i, k: (layer, k, 0))],
        out_specs=pl.BlockSpec((bm, N), lambda i, k: (i, 0)),
        out_shape=jax.ShapeDtypeStruct((M, N), F32),
        compiler_params=_params("parallel", "arbitrary"), name=name)(x, w)


def _gates_body(t_ref, bf_ref, w2_ref, bg_ref, logf_ref, la_ref):
    t = t_ref[...]
    logf_ref[...] = _log_sigmoid(t + bf_ref[...])
    zg = jnp.dot(t, w2_ref[...], preferred_element_type=F32) + bg_ref[...]
    la_ref[...] = _log_sigmoid(zg) * (1.0 / GLA_TAU)


def _gates(tail, bfox_pad, w2_pad, bgla):
    M = tail.shape[0]
    KW = w2_pad.shape[1]
    bm = _pick(M, 512)
    return pl.pallas_call(
        _gates_body, grid=(M // bm,),
        in_specs=[pl.BlockSpec((bm, LANES), lambda i: (i, 0)),
                  pl.BlockSpec((1, LANES), lambda i: (0, 0)),
                  pl.BlockSpec((LANES, KW), lambda i: (0, 0)),
                  pl.BlockSpec((1, KW), lambda i: (0, 0))],
        out_specs=[pl.BlockSpec((bm, LANES), lambda i: (i, 0)),
                   pl.BlockSpec((bm, KW), lambda i: (i, 0))],
        out_shape=[jax.ShapeDtypeStruct((M, LANES), F32), jax.ShapeDtypeStruct((M, KW), F32)],
        compiler_params=_params("parallel"), name="gates")(tail, bfox_pad, w2_pad, bgla)


def _row_cumsum(x):
    n = x.shape[0]
    row = lax.broadcasted_iota(jnp.int32, x.shape, 0)
    sh = 1
    while sh < n:
        x = x + jnp.where(row >= sh, pltpu.roll(x, sh, axis=0), 0.0)
        sh *= 2
    return x


def _cumsum_body(x_ref, c_ref, carry_sc):
    @pl.when(pl.program_id(1) == 0)
    def _():
        carry_sc[...] = jnp.zeros(carry_sc.shape, F32)

    c = _row_cumsum(x_ref[...]) + carry_sc[...]
    c_ref[...] = c
    n = c.shape[0]
    carry_sc[...] = c[n - 1:n, :]


def _cumsum(x, bl):
    S, L, W = x.shape
    bl = _pick(L, bl)
    spec = pl.BlockSpec((None, bl, W), lambda s, i: (s, i, 0))
    return pl.pallas_call(
        _cumsum_body, grid=(S, L // bl), in_specs=[spec], out_specs=spec,
        out_shape=jax.ShapeDtypeStruct((S, L, W), F32),
        scratch_shapes=[pltpu.VMEM((1, W), F32)],
        compiler_params=_params("parallel", "arbitrary"), name="cumsum")(x)


_NT = (((1,), (1,)), ((), ()))
_TN = (((0,), (0,)), ((), ()))


def _fox_prompt_body(q_ref, k_ref, vt_ref, b_ref, o_ref, acc_sc, st_sc, *, tq, tk, nsub):
    qi = pl.program_id(1)
    hd = k_ref.shape[1]
    bq = nsub * tq
    nkb = bq // tk
    ones = jnp.ones((ONES_ROWS, tk), BF16)
    rep = tq // LANES

    def scores(j, sub, mask_off):
        s0 = pl.multiple_of(j * tk, tk)
        q = q_ref[sub * tq:(sub + 1) * tq, :]
        k = k_ref[pl.ds(s0, tk), :]
        st = lax.dot_general(k, q, _NT, preferred_element_type=F32)
        b = b_ref[pl.ds(s0, tk), :]
        st = st + jnp.concatenate([b] * rep, axis=1)
        if mask_off is not None:
            key = lax.broadcasted_iota(jnp.int32, (tk, tq), 0) + mask_off
            qry = lax.broadcasted_iota(jnp.int32, (tk, tq), 1)
            st = jnp.where(key <= qry, st, -jnp.inf)
        return st

    def mask(st, mask_off):
        key = lax.broadcasted_iota(jnp.int32, (tk, tq), 0) + mask_off
        qry = lax.broadcasted_iota(jnp.int32, (tk, tq), 1)
        return jnp.where(key <= qry, st, -jnp.inf)

    def scores(j, sub):
        s0 = pl.multiple_of(j * tk, tk)
        q = q_ref[sub * tq:(sub + 1) * tq, :]
        k = k_ref[pl.ds(s0, tk), :]
        st = lax.dot_general(k, q, _NT, preferred_element_type=F32)
        b = b_ref[pl.ds(s0, tk), :]
        return st + jnp.concatenate([b] * rep, axis=1)

    def accumulate(j, sub, st, m_blk, m_prev):
        s0 = pl.multiple_of(j * tk, tk)
        m_new = jnp.maximum(m_prev, m_blk)
        alpha = jnp.exp2(m_prev - m_new)
        pt = jnp.exp2(st - m_new).astype(BF16)
        vt = jnp.concatenate([vt_ref[:, pl.ds(s0, tk)], ones], axis=0)
        acc_sc[sub] = alpha * acc_sc[sub] + jnp.dot(vt, pt, preferred_element_type=F32)
        return m_new

    def issue(j, slot):
        mb = []
        for sub in range(nsub):
            st = scores(j, sub)
            st_sc[slot, sub] = st
            mb.append(jnp.max(st, axis=0, keepdims=True))
        return tuple(mb)

    acc_sc[...] = jnp.zeros(acc_sc.shape, F32)
    m0 = tuple(jnp.full((1, tq), -jnp.inf, F32) for _ in range(nsub))
    n = nkb * qi

    def body(jj, carry):
        ms, mb = carry
        for slot in range(2):
            j = 2 * jj + slot
            mb_next = issue(j + 1, 1 - slot)
            ms = tuple(accumulate(j, sub, st_sc[slot, sub], mb[sub], ms[sub]) for sub in range(nsub))
            mb = mb_next
        return ms, mb

    assert nkb % 2 == 0
    ms, _ = lax.fori_loop(0, n // 2, body, (m0, issue(0, 0)))
    ms = list(ms)
    for d in range(nkb):
        for sub in range(nsub):
            k_lo, k_hi = d * tk, (d + 1) * tk - 1
            q_lo, q_hi = sub * tq, (sub + 1) * tq - 1
            if k_lo > q_hi:
                continue
            st = st_sc[0, sub] if d == 0 else scores(n + d, sub)
            if k_hi > q_lo:
                st = mask(st, k_lo - q_lo)
            ms[sub] = accumulate(n + d, sub, st, jnp.max(st, axis=0, keepdims=True), ms[sub])
    for sub in range(nsub):
        acc = acc_sc[sub]
        o = acc[:hd, :] / acc[hd:hd + 1, :]
        o_ref[sub * tq:(sub + 1) * tq, :] = jnp.transpose(o).astype(o_ref.dtype)


FOX_TQ, FOX_TK, FOX_NSUB = 512, 512, 2


def _fox_prompt(q16, k16, vt16, brep, T):
    M, W = q16.shape
    H = W // FOX_HD
    tq, tk, nsub = FOX_TQ, FOX_TK, FOX_NSUB
    bq = nsub * tq
    assert T % bq == 0 and bq % tk == 0
    return pl.pallas_call(
        functools.partial(_fox_prompt_body, tq=tq, tk=tk, nsub=nsub), grid=(H, T // bq),
        in_specs=[pl.BlockSpec((bq, FOX_HD), lambda h, i: (i, h)),
                  pl.BlockSpec((T, FOX_HD), lambda h, i: (0, h)),
                  pl.BlockSpec((FOX_HD, T), lambda h, i: (h, 0)),
                  pl.BlockSpec((None, T, LANES), lambda h, i: (h, 0, 0))],
        out_specs=pl.BlockSpec((bq, FOX_HD), lambda h, i: (i, h)),
        out_shape=jax.ShapeDtypeStruct((M, W), BF16),
        scratch_shapes=[pltpu.VMEM((nsub, FOX_HD + ONES_ROWS, tq), F32),
                        pltpu.VMEM((2, nsub, tk, tq), F32)],
        compiler_params=_params("parallel", "arbitrary"), name="fox_prompt")(q16, k16, vt16, brep)


def _fox_sample_body(q_ref, kn_ref, vn_ref, kp_ref, vp_ref, c_ref, fo_any, o_ref, *, P, ST, HB):
    del fo_any
    r = lax.broadcasted_iota(jnp.int32, (ST, ST), 0)
    cc = lax.broadcasted_iota(jnp.int32, (ST, ST), 1)
    hd = FOX_HD
    kp = pltpu.einshape("phd->hpd", kp_ref[...]).astype(BF16)
    vp = pltpu.einshape("phd->hpd", vp_ref[...]).astype(BF16)
    scores = []
    for h in range(HB):
        sl = slice(h * hd, (h + 1) * hd)
        q = q_ref[:, sl]
        c = c_ref[h]
        cb = c[:, P - 1:P]
        sp = (lax.dot_general(q, kp[h], _NT, preferred_element_type=F32)
              + (cb - c[:, :P]) * LOG2E)
        sn = (lax.dot_general(q, kn_ref[:, sl], _NT, preferred_element_type=F32)
              + (cb - c[:, P:]) * LOG2E)
        scores.append((sp, jnp.where(cc <= r, sn, -jnp.inf)))
    for h, (sp, sn) in enumerate(scores):
        sl = slice(h * hd, (h + 1) * hd)
        m = jnp.maximum(jnp.max(sp, axis=1, keepdims=True), jnp.max(sn, axis=1, keepdims=True))
        pp = jnp.exp2(sp - m)
        pn = jnp.exp2(sn - m)
        l = jnp.sum(pp, axis=1, keepdims=True) + jnp.sum(pn, axis=1, keepdims=True)
        o = (jnp.dot(pp.astype(BF16), vp[h], preferred_element_type=F32)
             + jnp.dot(pn.astype(BF16), vn_ref[:, sl], preferred_element_type=F32))
        o_ref[:, sl] = (o / l).astype(o_ref.dtype)


def _fox_sample(fo, q16, k16, v16, cache_k, cache_v, ct, layer, T, SB, ST):
    M, W = q16.shape
    H = W // FOX_HD
    P = cache_k.shape[2]
    r0 = T // ST
    HB = SUBLANES
    assert H % HB == 0
    new = pl.BlockSpec((ST, HB * FOX_HD), lambda b, h: (r0 + b, h))
    past = pl.BlockSpec((None, None, P, HB, FOX_HD), lambda b, h: (layer, b, 0, h, 0))
    return pl.pallas_call(
        functools.partial(_fox_sample_body, P=P, ST=ST, HB=HB), grid=(SB, H // HB),
        in_specs=[new, new, new, past, past,
                  pl.BlockSpec((None, HB, 1, P + ST), lambda b, h: (b, h, 0, 0)),
                  pl.BlockSpec(memory_space=pl.ANY)],
        out_specs=new,
        out_shape=jax.ShapeDtypeStruct((M, W), BF16),
        input_output_aliases={6: 0},
        compiler_params=_params("parallel", "arbitrary"), name="fox_sample")(
            q16, k16, v16, cache_k, cache_v, ct, fo)


def _gla_body(q_ref, k_ref, v_ref, r_ref, la_ref, s0_ref, g_ref, go_any, o_ref, sout_ref,
              S_sc, b_sc, q_sc, *, C, NH, DK, DV, nc):
    del go_any
    c = pl.program_id(1)

    @pl.when(c == 0)
    def _():
        S_sc[...] = s0_ref[...]

    b = _row_cumsum(la_ref[...])
    b_sc[...] = b
    q = q_ref[...] * (DK ** -0.5)
    q_sc[...] = q
    bl = b[C - 1:C, :]
    qe = (q * jnp.exp(b)).astype(BF16)
    kdec = (k_ref[...] * jnp.exp(bl - b)).astype(BF16)
    ebl = jnp.exp(bl)
    v16 = v_ref[...].astype(BF16)

    R = GLA_SUB
    nb = C // R
    k = k_ref[...]
    srow_c = lax.broadcasted_iota(jnp.int32, b.shape, 0)
    at_off = [None] * NH
    for i in range(1, nb):
        b_ref_i = b[R * i - 1:R * i, :]
        in_blk = (srow_c >> GLA_SUB_LOG2) == i
        qt = (q * jnp.exp(jnp.where(in_blk, b - b_ref_i, -jnp.inf))).astype(BF16)
        kt = (k * jnp.exp(jnp.where(srow_c < R * i, b_ref_i - b, -jnp.inf))).astype(BF16)
        for h in range(NH):
            sl = slice(h * DK, (h + 1) * DK)
            part = lax.dot_general(kt[:, sl], qt[:, sl], _NT, preferred_element_type=F32)
            at_off[h] = part if at_off[h] is None else at_off[h] + part

    srow = lax.broadcasted_iota(jnp.int32, (R, DK), 0)
    lane = lax.broadcasted_iota(jnp.int32, (R, C), 1)
    at_diag = [[] for _ in range(NH)]
    for i in range(nb):
        blk = slice(R * i, R * (i + 1))
        ats = [jnp.zeros((R, C), F32) for _ in range(NH)]
        for r in range(R):
            t = R * i + r
            mask = srow <= r
            lm = lane == t
            for h in range(NH):
                sl = slice(h * DK, (h + 1) * DK)
                e = jnp.exp(jnp.where(mask, b_sc[t:t + 1, sl] - b_sc[blk, sl], -jnp.inf))
                w = e * (q_sc[t:t + 1, sl] * k_ref[blk, sl])
                ats[h] = jnp.where(lm, jnp.sum(w, axis=1, keepdims=True), ats[h])
        for h in range(NH):
            at_diag[h].append(ats[h])
    ats = []
    for h in range(NH):
        at = jnp.concatenate(at_diag[h], axis=0)
        ats.append(at if at_off[h] is None else at + at_off[h])

    g = g_ref[...]
    for h in range(NH):
        sl = slice(h * DK, (h + 1) * DK)
        vs = slice(h * DV, (h + 1) * DV)
        S = S_sc[h]
        o = (jnp.dot(qe[:, sl], S.astype(BF16), preferred_element_type=F32)
             + lax.dot_general(ats[h].astype(BF16), v16[:, vs], _TN, preferred_element_type=F32))
        r = r_ref[:, vs]
        o = _rms(o) * g * (r * (1.0 / (1.0 + jnp.exp(-r))))
        o_ref[:, vs] = o.astype(o_ref.dtype)
        dm = jnp.transpose(jnp.broadcast_to(ebl[:, sl], (DK, DK)))
        decay = jnp.concatenate([dm] * (DV // DK), axis=1)
        S_sc[h] = decay * S + lax.dot_general(kdec[:, sl], v16[:, vs], _TN, preferred_element_type=F32)

    @pl.when(c == nc - 1)
    def _():
        sout_ref[...] = S_sc[...]


def _gla(rest, la, s0, g, go, *, row0, nseq, nc, C, DK, DV):
    M = rest.shape[0]
    NH = GLA_HEADS
    KW, VW = NH * DK, NH * DV
    r0 = row0 // C
    assert row0 % C == 0 and VW == 2 * KW

    def rows(col):
        return lambda s, c: (r0 + s * nc + c, col)

    in_specs = [pl.BlockSpec((C, KW), rows(0)), pl.BlockSpec((C, KW), rows(1)),
                pl.BlockSpec((C, VW), rows(1)), pl.BlockSpec((C, VW), rows(2)),
                pl.BlockSpec((C, KW), rows(0)),
                pl.BlockSpec((None, NH, DK, DV), lambda s, c: (s, 0, 0, 0)),
                pl.BlockSpec((1, DV), lambda s, c: (0, 0))]
    args = [rest, rest, rest, rest, la, s0, g.reshape(1, DV)]
    aliases = {}
    if go is not None:
        in_specs.append(pl.BlockSpec(memory_space=pl.ANY))
        args.append(go)
        aliases = {7: 0}
        body = _gla_body
    else:
        body = functools.partial(_gla_body_noalias)
    return pl.pallas_call(
        functools.partial(body, C=C, NH=NH, DK=DK, DV=DV, nc=nc), grid=(nseq, nc),
        in_specs=in_specs,
        out_specs=[pl.BlockSpec((C, VW), rows(0)),
                   pl.BlockSpec((None, NH, DK, DV), lambda s, c: (s, 0, 0, 0))],
        out_shape=[jax.ShapeDtypeStruct((M, VW), BF16),
                   jax.ShapeDtypeStruct((nseq, NH, DK, DV), F32)],
        scratch_shapes=[pltpu.VMEM((NH, DK, DV), F32), pltpu.VMEM((C, KW), F32),
                        pltpu.VMEM((C, KW), F32)],
        input_output_aliases=aliases,
        compiler_params=_params("parallel", "arbitrary"), name="gla")(*args)


def _gla_body_noalias(q_ref, k_ref, v_ref, r_ref, la_ref, s0_ref, g_ref, o_ref, sout_ref,
                      S_sc, b_sc, q_sc, **kw):
    _gla_body(q_ref, k_ref, v_ref, r_ref, la_ref, s0_ref, g_ref, None, o_ref, sout_ref,
              S_sc, b_sc, q_sc, **kw)


def _pool_body(*refs, bm, pos0, GC, aliased):
    if aliased:
        p_ref, hist_ref, w_ref, sp_ref, _, o_ref, hout_ref, xa_sc = refs
    else:
        p_ref, hist_ref, w_ref, sp_ref, o_ref, hout_ref, xa_sc = refs
    i = pl.program_id(1)

    @pl.when(i == 0)
    def _():
        xa_sc[0:POOL_CARRY, :] = hist_ref[...]

    xa_sc[POOL_CARRY:POOL_CARRY + bm, :] = p_ref[...]
    pos = pos0 + i * bm + lax.broadcasted_iota(jnp.int32, (bm, 1), 0)
    for gi, w in enumerate(POOL_WINDOWS):
        cs = slice(gi * GC, (gi + 1) * GC)
        x = xa_sc[POOL_CARRY:POOL_CARRY + bm, cs]
        tot = x
        for d in range(1, w):
            tot = tot + xa_sc[POOL_CARRY - d:POOL_CARRY - d + bm, cs]
        cnt = jnp.minimum(pos + 1, w).astype(F32)
        dd = (tot / cnt - x).astype(BF16)
        y = jnp.dot(dd, w_ref[gi], preferred_element_type=F32) * sp_ref[:, cs]
        o_ref[:, cs] = y.astype(o_ref.dtype)
    last = xa_sc[bm:bm + POOL_CARRY, :]
    hout_ref[...] = last
    xa_sc[0:POOL_CARRY, :] = last


def _pool(rest, col, hist, w16, sp, po, *, row0, nseq, nb, bm, pos0):
    M = rest.shape[0]
    G, GC, _ = w16.shape
    PW = G * GC
    r0 = row0 // bm
    assert row0 % bm == 0 and bm >= POOL_CARRY
    in_specs = [pl.BlockSpec((bm, PW), lambda s, i: (r0 + s * nb + i, col)),
                pl.BlockSpec((None, POOL_CARRY, PW), lambda s, i: (s, 0, 0)),
                pl.BlockSpec((G, GC, GC), lambda s, i: (0, 0, 0)),
                pl.BlockSpec((1, PW), lambda s, i: (0, 0))]
    args = [rest, hist, w16, sp.reshape(1, PW)]
    aliases = {}
    if po is not None:
        in_specs.append(pl.BlockSpec(memory_space=pl.ANY))
        args.append(po)
        aliases = {4: 0}
    return pl.pallas_call(
        functools.partial(_pool_body, bm=bm, pos0=pos0, GC=GC, aliased=po is not None),
        grid=(nseq, nb), in_specs=in_specs,
        out_specs=[pl.BlockSpec((bm, PW), lambda s, i: (r0 + s * nb + i, 0)),
                   pl.BlockSpec((None, POOL_CARRY, PW), lambda s, i: (s, 0, 0))],
        out_shape=[jax.ShapeDtypeStruct((M, PW), BF16),
                   jax.ShapeDtypeStruct((nseq, POOL_CARRY, PW), F32)],
        scratch_shapes=[pltpu.VMEM((POOL_CARRY + bm, PW), F32)],
        input_output_aliases=aliases,
        compiler_params=_params("parallel", "arbitrary"), name="pool")(*args)


def kernel(x_prompt, x_sample, cache_fox_k, cache_fox_v, cache_fox_logf, state_gla, state_pool, g_pre_mix, w_in, b_fox_f, w_gla_a2, b_gla_a, g_gla_norm, w_pool, s_pool, w_out, g_post_mix, g_pre_mlp, w_up, w_down, g_post_mlp):
    B, T, D = x_prompt.shape
    SB, ST, _ = x_sample.shape
    L = w_in.shape[0]
    P = cache_fox_k.shape[2]
    assert B == 1
    FW = D // 2
    H = FW // FOX_HD
    GW = D // 4
    DV = GW // GLA_HEADS
    DK = DV // 2
    KW = GLA_HEADS * DK
    RANK = w_gla_a2.shape[1]
    PW = D - FW - GW
    GC = PW // len(POOL_WINDOWS)
    MS = SB * ST
    M = T + MS
    assert H + RANK <= LANES and 2 * KW + 2 * GW == 3 * PW and PW % LANES == 0

    o_fq, o_fk, o_fv, o_ff = 0, FW, 2 * FW, 3 * FW
    o_gq = o_ff + H
    o_ga = o_gq + 2 * KW + GW
    o_gr = o_ga + RANK

    w_in16 = w_in.astype(BF16)
    w_rest16 = jnp.concatenate([w_in16[:, :, o_gq:o_ga], w_in16[:, :, o_gr:]], axis=2)
    w_tail16 = jnp.concatenate([w_in16[:, :, o_ff:o_gq], w_in16[:, :, o_ga:o_gr],
                                jnp.zeros((L, D, LANES - H - RANK), BF16)], axis=2)
    w_out16, w_up16, w_down16, w_pool16 = (w.astype(BF16) for w in (w_out, w_up, w_down, w_pool))
    past_logf = jnp.pad(cache_fox_logf.astype(F32), ((0, 0), (0, 0), (0, 0), (0, LANES - H)))
    zero_state = jnp.zeros((1, GLA_HEADS, DK, DV), F32)
    zero_hist = jnp.zeros((1, POOL_CARRY, PW), F32)
    sample_hist = jnp.pad(state_pool, ((0, 0), (0, 0), (POOL_CARRY - POOL_HIST, 0), (0, 0)))

    h = jnp.concatenate([x_prompt.reshape(T, D), x_sample.reshape(MS, D)], axis=0)
    a = _norm_first(h, g_pre_mix[0])
    outs_p = [[] for _ in range(5)]
    outs_s = [[] for _ in range(5)]
    kv_p, kv_s = [None, None], [None, None]
    for l in range(L):
        bfox_pad = jnp.pad(b_fox_f[l].astype(F32), (0, LANES - H)).reshape(1, LANES)
        w2_pad = jnp.pad(w_gla_a2[l].astype(F32), ((H, LANES - H - RANK), (0, 0)))

        (q16,) = _matmul(a, w_in16, l, [BF16], name="proj_q", col0=o_fq, ncols=FW,
                         scale=FOX_HD ** -0.5 * LOG2E)
        kv16 = []
        for t, (name, col0) in enumerate((("proj_k", o_fk), ("proj_v", o_fv))):
            common = dict(name=name, col0=col0, ncols=FW, layer=l, nlayers=L, hd=FOX_HD)
            kv_p[t], x16 = _matmul_heads(a, w_in16, kv_p[t], None, row0=0, nrows=T, **common)
            kv_s[t], x16 = _matmul_heads(a, w_in16, kv_s[t], x16, row0=T, nrows=MS, **common)
            kv16.append(x16)
        k16, v16 = kv16
        (rest,) = _matmul(a, w_rest16, l, [F32], name="proj_rest")
        (tail,) = _matmul(a, w_tail16, l, [F32], name="proj_tail")
        logf, la = _gates(tail, bfox_pad, w2_pad, b_gla_a[l].reshape(1, KW))

        c_p = _cumsum(logf[:T].reshape(1, T, LANES), 1024)
        brep = jnp.broadcast_to(jnp.transpose(c_p[0, :, :H] * (-LOG2E))[:, :, None], (H, T, LANES))
        vt16 = jnp.transpose(v16[:T])
        fo = _fox_prompt(q16, k16, vt16, brep, T)
        lf_all = jnp.concatenate([past_logf[l], logf[T:].reshape(SB, ST, LANES)], axis=1)
        c_s = _cumsum(lf_all, P + ST)
        ct_s = jnp.transpose(c_s[:, :, :H], (0, 2, 1)).reshape(SB, H, 1, P + ST)
        fo = _fox_sample(fo, q16, k16, v16, cache_fox_k, cache_fox_v, ct_s, l, T, SB, ST)

        go, gla_p = _gla(rest, la, zero_state, g_gla_norm[l], None,
                         row0=0, nseq=1, nc=T // GLA_CHUNK, C=GLA_CHUNK, DK=DK, DV=DV)
        go, gla_s = _gla(rest, la, state_gla[l], g_gla_norm[l], go,
                         row0=T, nseq=SB, nc=1, C=ST, DK=DK, DV=DV)

        pcol = (2 * KW + 2 * GW) // PW
        bp = _pick(T, 512)
        po, hist_p = _pool(rest, pcol, zero_hist, w_pool16[l], s_pool[l], None,
                           row0=0, nseq=1, nb=T // bp, bm=bp, pos0=0)
        po, hist_s = _pool(rest, pcol, sample_hist[l], w_pool16[l], s_pool[l], po,
                           row0=T, nseq=SB, nb=1, bm=ST, pos0=P)

        y = _matmul_cat([fo, go, po], w_out16, l, name="proj_out")
        h, c = _norm_mid(h, y, g_post_mix[l], g_pre_mlp[l])
        (hid,) = _matmul(c, w_up16, l, [BF16], name="mlp_up", relu2=True)
        f = _matmul_kacc(hid, w_down16, l, name="mlp_down")
        if l + 1 < L:
            h, a = _norm_mid(h, f, g_post_mlp[l], g_pre_mix[l + 1])
        else:
            y_prompt = _norm_last(h, f, g_post_mlp[l], 0, T)
            y_sample = _norm_last(h, f, g_post_mlp[l], T, MS)

        outs_p[2].append(logf[:T, :H].reshape(1, T, H))
        outs_p[3].append(gla_p)
        outs_p[4].append(hist_p[:, POOL_CARRY - POOL_HIST:])
        outs_s[2].append(logf[T:, :H].reshape(SB, ST, H))
        outs_s[3].append(gla_s)
        outs_s[4].append(hist_s[:, POOL_CARRY - POOL_HIST:])

    p_kv = [x.reshape(L, 1, T, H, FOX_HD) for x in kv_p]
    s_kv = [x.reshape(L, SB, ST, H, FOX_HD) for x in kv_s]
    return (y_prompt.reshape(1, T, D), y_sample.reshape(SB, ST, D),
            *p_kv, *[jnp.stack(o) for o in outs_p[2:]], *s_kv, *[jnp.stack(o) for o in outs_s[2:]])
```

```python
import functools
import math

import jax
import jax.numpy as jnp
from jax import lax
from jax.experimental import pallas as pl
from jax.experimental.pallas import tpu as pltpu

F32 = jnp.float32
BF16 = jnp.bfloat16

RMS_EPS = 1e-6
FOX_HD = 128
GLA_HEADS = 4
GLA_TAU = 16.0
GLA_CHUNK = 64
GLA_SUB_LOG2 = 4
GLA_SUB = 1 << GLA_SUB_LOG2
POOL_WINDOWS = (2, 4, 8, 16)
POOL_HIST = 15
POOL_CARRY = 16
LANES = 128
SUBLANES = 8
ONES_ROWS = 16
LOG2E = 1.4426950408889634
VMEM_LIMIT = 56 * 1024 * 1024


def _params(*sem):
    return pltpu.CompilerParams(dimension_semantics=sem, vmem_limit_bytes=VMEM_LIMIT)


def _pick(n, pref, mult=8):
    b = min(pref, n)
    b -= b % mult
    while b > 0 and n % b:
        b -= mult
    assert b > 0, (n, pref, mult)
    return b


def _rms(x):
    return x * lax.rsqrt(jnp.mean(x * x, axis=-1, keepdims=True) + RMS_EPS)


def _log_sigmoid(x):
    return jnp.minimum(x, 0.0) - jnp.log1p(jnp.exp(-jnp.abs(x)))


def _norm_first_body(h_ref, g_ref, a_ref):
    a_ref[...] = (_rms(h_ref[...]) * g_ref[...]).astype(a_ref.dtype)


def _norm_mid_body(h_ref, y_ref, gy_ref, g_ref, h_out, a_out):
    h = h_ref[...] + _rms(y_ref[...]) * gy_ref[...]
    h_out[...] = h
    a_out[...] = (_rms(h) * g_ref[...]).astype(a_out.dtype)


def _norm_last_body(h_ref, y_ref, gy_ref, h_out):
    h_out[...] = h_ref[...] + _rms(y_ref[...]) * gy_ref[...]


def _row_block(row0, nrows, pref):
    return _pick(math.gcd(row0, nrows) if row0 else nrows, pref)


def _norm_first(x, g, prev, row0, M):
    nrows, D = x.shape
    bm = _row_block(row0, nrows, 256)
    r0 = row0 // bm
    in_specs = [pl.BlockSpec((bm, D), lambda i: (i, 0)), pl.BlockSpec((1, D), lambda i: (0, 0))]
    args = [x, g.reshape(1, D)]
    if prev is not None:
        in_specs.append(pl.BlockSpec(memory_space=pl.ANY))
        args.append(prev)
    return pl.pallas_call(
        lambda x_ref, g_ref, *rest: _norm_first_body(x_ref, g_ref, rest[-1]),
        grid=(nrows // bm,), in_specs=in_specs, out_specs=pl.BlockSpec((bm, D), lambda i: (r0 + i, 0)),
        out_shape=jax.ShapeDtypeStruct((M, D), BF16),
        input_output_aliases={2: 0} if prev is not None else {},
        compiler_params=_params("parallel"), name="norm_first")(*args)


def _norm_mid(h_src, y, gy, g, prev=None, row0=0):
    nrows, D = h_src.shape
    M = y.shape[0]
    bm = _row_block(row0, nrows, 256)
    r0 = row0 // bm
    src = pl.BlockSpec((bm, D), lambda i: (i, 0))
    row = pl.BlockSpec((bm, D), lambda i: (r0 + i, 0))
    vec = pl.BlockSpec((1, D), lambda i: (0, 0))
    in_specs = [src, row, vec, vec]
    args = [h_src, y, gy.reshape(1, D), g.reshape(1, D)]
    if prev is not None:
        in_specs += [pl.BlockSpec(memory_space=pl.ANY)] * 2
        args += list(prev)
    return pl.pallas_call(
        lambda h_ref, y_ref, gy_ref, g_ref, *rest: _norm_mid_body(h_ref, y_ref, gy_ref, g_ref, *rest[-2:]),
        grid=(nrows // bm,), in_specs=in_specs, out_specs=[row, row],
        out_shape=[jax.ShapeDtypeStruct((M, D), F32), jax.ShapeDtypeStruct((M, D), BF16)],
        input_output_aliases={4: 0, 5: 1} if prev is not None else {},
        compiler_params=_params("parallel"), name="norm_mid")(*args)


def _norm_last(h, y, gy, row0, nrows):
    D = h.shape[1]
    bm = _pick(math.gcd(row0, nrows) if row0 else nrows, 256)
    r0 = row0 // bm
    row = pl.BlockSpec((bm, D), lambda i: (r0 + i, 0))
    vec = pl.BlockSpec((1, D), lambda i: (0, 0))
    return pl.pallas_call(
        _norm_last_body, grid=(nrows // bm,), in_specs=[row, row, vec],
        out_specs=pl.BlockSpec((bm, D), lambda i: (i, 0)),
        out_shape=jax.ShapeDtypeStruct((nrows, D), F32),
        compiler_params=_params("parallel"), name="norm_last")(h, y, gy.reshape(1, D))


def _mm_body(x_ref, w_ref, *o_refs, relu2, scale):
    w = w_ref[...]
    if w.dtype != BF16:
        w = w.astype(BF16)
    acc = jnp.dot(x_ref[...], w, preferred_element_type=F32)
    if relu2:
        acc = jnp.square(jnp.maximum(acc, 0.0))
    if scale is not None:
        acc = acc * scale
    for o in o_refs:
        o[...] = acc.astype(o.dtype)


def _matmul(x, w, layer, out_dtypes, *, name, relu2=False, scale=None, col0=0, ncols=None, bm=1024, bn=1024):
    M, K = x.shape
    N = w.shape[2] - col0 if ncols is None else ncols
    bm = _pick(M, bm)
    bn = _pick(N, bn, LANES)
    assert col0 % bn == 0
    c0 = col0 // bn
    outs = pl.pallas_call(
        functools.partial(_mm_body, relu2=relu2, scale=scale), grid=(M // bm, N // bn),
        in_specs=[pl.BlockSpec((bm, K), lambda i, j: (i, 0)),
                  pl.BlockSpec((None, K, bn), lambda i, j: (layer, 0, c0 + j))],
        out_specs=[pl.BlockSpec((bm, bn), lambda i, j: (i, j)) for _ in out_dtypes],
        out_shape=[jax.ShapeDtypeStruct((M, N), dt) for dt in out_dtypes],
        compiler_params=_params("parallel", "arbitrary"), name=name)(x, w)
    return outs


def _mm_heads_body(x_ref, w_ref, *refs, hd):
    o32_ref, o16_ref = refs[-2:]
    acc = jnp.dot(x_ref[...], w_ref[...], preferred_element_type=F32)
    o16_ref[...] = acc.astype(o16_ref.dtype)
    o32_ref[...] = pltpu.einshape("m(hd)->mhd", acc, d=hd)


def _matmul_heads(x, w, prev32, prev16, *, name, col0, ncols, row0, nrows, layer, nlayers, hd, bm=512, bn=1024):
    M, K = x.shape
    bm = _pick(math.gcd(row0, nrows) if row0 else nrows, bm)
    bn = _pick(ncols, bn, LANES * SUBLANES)
    assert col0 % bn == 0 and bn % hd == 0
    c0, r0, hb = col0 // bn, row0 // bm, bn // hd
    in_specs = [pl.BlockSpec((bm, K), lambda i, j: (r0 + i, 0)),
                pl.BlockSpec((None, K, bn), lambda i, j: (layer, 0, c0 + j))]
    args = [x, w]
    aliases = {}
    for out_idx, prev in enumerate((prev32, prev16)):
        if prev is not None:
            aliases[len(args)] = out_idx
            in_specs.append(pl.BlockSpec(memory_space=pl.ANY))
            args.append(prev)
    return pl.pallas_call(
        functools.partial(_mm_heads_body, hd=hd), grid=(nrows // bm, ncols // bn), in_specs=in_specs,
        out_specs=[pl.BlockSpec((None, bm, hb, hd), lambda i, j: (layer, i, j, 0)),
                   pl.BlockSpec((bm, bn), lambda i, j: (r0 + i, j))],
        out_shape=[jax.ShapeDtypeStruct((nlayers, nrows, ncols // hd, hd), F32),
                   jax.ShapeDtypeStruct((M, ncols), BF16)],
        input_output_aliases=aliases,
        compiler_params=_params("parallel", "arbitrary"), name=name)(*args)


def _mm_cat_body(*refs, splits):
    x_refs, w_ref, o_ref = refs[:-2], refs[-2], refs[-1]
    acc = None
    off = 0
    for x_ref, kk in zip(x_refs, splits):
        part = jnp.dot(x_ref[...], w_ref[off:off + kk, :], preferred_element_type=F32)
        acc = part if acc is None else acc + part
        off += kk
    o_ref[...] = acc


def _matmul_cat(xs, w, layer, *, name, bm=1024, bn=1024):
    M = xs[0].shape[0]
    _, K, N = w.shape
    splits = tuple(x.shape[1] for x in xs)
    assert sum(splits) == K
    bm = _pick(M, bm)
    bn = _pick(N, bn, LANES)
    in_specs = [pl.BlockSpec((bm, kk), lambda i, j: (i, 0)) for kk in splits]
    in_specs.append(pl.BlockSpec((None, K, bn), lambda i, j: (layer, 0, j)))
    return pl.pallas_call(
        functools.partial(_mm_cat_body, splits=splits), grid=(M // bm, N // bn),
        in_specs=in_specs, out_specs=pl.BlockSpec((bm, bn), lambda i, j: (i, j)),
        out_shape=jax.ShapeDtypeStruct((M, N), F32),
        compiler_params=_params("parallel", "arbitrary"), name=name)(*xs, w)


def _mm_kacc_body(x_ref, w_ref, o_ref, *, bn):
    k = pl.program_id(1)

    @pl.when(k == 0)
    def _():
        o_ref[...] = jnp.zeros(o_ref.shape, F32)

    x = x_ref[...]
    for n0 in range(0, o_ref.shape[1], bn):
        o_ref[:, n0:n0 + bn] += jnp.dot(x, w_ref[:, n0:n0 + bn], preferred_element_type=F32)


def _matmul_kacc(x, w, layer, *, name, bm=512, bk=1024):
    M, K = x.shape
    N = w.shape[2]
    bm = _pick(M, bm)
    bk = _pick(K, bk, LANES)
    return pl.pallas_call(
        functools.partial(_mm_kacc_body, bn=_pick(N, 512, LANES)), grid=(M // bm, K // bk),
        in_specs=[pl.BlockSpec((bm, bk), lambda i, k: (i, k)),
                  pl.BlockSpec((None, bk, N), lambda i, k: (layer, k, 0))],
        out_specs=pl.BlockSpec((bm, N), lambda i, k: (i, 0)),
        out_shape=jax.ShapeDtypeStruct((M, N), F32),
        compiler_params=_params("parallel", "arbitrary"), name=name)(x, w)


def _gates_body(t_ref, bf_ref, w2_ref, bg_ref, logf_ref, la_ref):
    t = t_ref[...]
    logf_ref[...] = _log_sigmoid(t + bf_ref[...])
    zg = jnp.dot(t, w2_ref[...], preferred_element_type=F32) + bg_ref[...]
    la_ref[...] = _log_sigmoid(zg) * (1.0 / GLA_TAU)


def _gates(tail, bfox_pad, w2_pad, bgla):
    M = tail.shape[0]
    KW = w2_pad.shape[1]
    bm = _pick(M, 512)
    return pl.pallas_call(
        _gates_body, grid=(M // bm,),
        in_specs=[pl.BlockSpec((bm, LANES), lambda i: (i, 0)),
                  pl.BlockSpec((1, LANES), lambda i: (0, 0)),
                  pl.BlockSpec((LANES, KW), lambda i: (0, 0)),
                  pl.BlockSpec((1, KW), lambda i: (0, 0))],
        out_specs=[pl.BlockSpec((bm, LANES), lambda i: (i, 0)),
                   pl.BlockSpec((bm, KW), lambda i: (i, 0))],
        out_shape=[jax.ShapeDtypeStruct((M, LANES), F32), jax.ShapeDtypeStruct((M, KW), F32)],
        compiler_params=_params("parallel"), name="gates")(tail, bfox_pad, w2_pad, bgla)


def _row_cumsum(x):
    n = x.shape[0]
    row = lax.broadcasted_iota(jnp.int32, x.shape, 0)
    sh = 1
    while sh < n:
        x = x + jnp.where(row >= sh, pltpu.roll(x, sh, axis=0), 0.0)
        sh *= 2
    return x


def _cumsum_body(x_ref, c_ref, carry_sc):
    @pl.when(pl.program_id(1) == 0)
    def _():
        carry_sc[...] = jnp.zeros(carry_sc.shape, F32)

    c = _row_cumsum(x_ref[...]) + carry_sc[...]
    c_ref[...] = c
    n = c.shape[0]
    carry_sc[...] = c[n - 1:n, :]


def _cumsum(x, bl):
    S, L, W = x.shape
    bl = _pick(L, bl)
    spec = pl.BlockSpec((None, bl, W), lambda s, i: (s, i, 0))
    return pl.pallas_call(
        _cumsum_body, grid=(S, L // bl), in_specs=[spec], out_specs=spec,
        out_shape=jax.ShapeDtypeStruct((S, L, W), F32),
        scratch_shapes=[pltpu.VMEM((1, W), F32)],
        compiler_params=_params("parallel", "arbitrary"), name="cumsum")(x)


_NT = (((1,), (1,)), ((), ()))
_TN = (((0,), (0,)), ((), ()))


def _fox_prompt_body(q_ref, k_ref, vt_ref, b_ref, o_ref, acc_sc, st_sc, *, tq, tk, nsub):
    qi = pl.program_id(1)
    hd = k_ref.shape[1]
    bq = nsub * tq
    nkb = bq // tk
    ones = jnp.ones((ONES_ROWS, tk), BF16)
    rep = tq // LANES

    def scores(j, sub, mask_off):
        s0 = pl.multiple_of(j * tk, tk)
        q = q_ref[sub * tq:(sub + 1) * tq, :]
        k = k_ref[pl.ds(s0, tk), :]
        st = lax.dot_general(k, q, _NT, preferred_element_type=F32)
        b = b_ref[pl.ds(s0, tk), :]
        st = st + jnp.concatenate([b] * rep, axis=1)
        if mask_off is not None:
            key = lax.broadcasted_iota(jnp.int32, (tk, tq), 0) + mask_off
            qry = lax.broadcasted_iota(jnp.int32, (tk, tq), 1)
            st = jnp.where(key <= qry, st, -jnp.inf)
        return st

    def mask(st, mask_off):
        key = lax.broadcasted_iota(jnp.int32, (tk, tq), 0) + mask_off
        qry = lax.broadcasted_iota(jnp.int32, (tk, tq), 1)
        return jnp.where(key <= qry, st, -jnp.inf)

    def scores(j, sub):
        s0 = pl.multiple_of(j * tk, tk)
        q = q_ref[sub * tq:(sub + 1) * tq, :]
        k = k_ref[pl.ds(s0, tk), :]
        st = lax.dot_general(k, q, _NT, preferred_element_type=F32)
        b = b_ref[pl.ds(s0, tk), :]
        return st + jnp.concatenate([b] * rep, axis=1)

    def accumulate(j, sub, st, m_blk, m_prev):
        s0 = pl.multiple_of(j * tk, tk)
        m_new = jnp.maximum(m_prev, m_blk)
        alpha = jnp.exp2(m_prev - m_new)
        pt = jnp.exp2(st - m_new).astype(BF16)
        vt = jnp.concatenate([vt_ref[:, pl.ds(s0, tk)], ones], axis=0)
        acc_sc[sub] = alpha * acc_sc[sub] + jnp.dot(vt, pt, preferred_element_type=F32)
        return m_new

    def issue(j, slot):
        mb = []
        for sub in range(nsub):
            st = scores(j, sub)
            st_sc[slot, sub] = st
            mb.append(jnp.max(st, axis=0, keepdims=True))
        return tuple(mb)

    acc_sc[...] = jnp.zeros(acc_sc.shape, F32)
    m0 = tuple(jnp.full((1, tq), -jnp.inf, F32) for _ in range(nsub))
    n = nkb * qi

    def body(jj, carry):
        ms, mb = carry
        for slot in range(2):
            j = 2 * jj + slot
            mb_next = issue(j + 1, 1 - slot)
            ms = tuple(accumulate(j, sub, st_sc[slot, sub], mb[sub], ms[sub]) for sub in range(nsub))
            mb = mb_next
        return ms, mb

    assert nkb % 2 == 0
    ms, _ = lax.fori_loop(0, n // 2, body, (m0, issue(0, 0)))
    ms = list(ms)
    band = [(d, sub) for d in range(nkb) for sub in range(nsub) if d * tk <= (sub + 1) * tq - 1]
    late = {(d, sub): scores(n + d, sub) for d, sub in band if d > 0}
    for d, sub in band:
        k_lo, k_hi, q_lo = d * tk, (d + 1) * tk - 1, sub * tq
        st = st_sc[0, sub] if d == 0 else late[d, sub]
        if k_hi > q_lo:
            st = mask(st, k_lo - q_lo)
        ms[sub] = accumulate(n + d, sub, st, jnp.max(st, axis=0, keepdims=True), ms[sub])
    for sub in range(nsub):
        acc = acc_sc[sub]
        o = acc[:hd, :] / acc[hd:hd + 1, :]
        o_ref[sub * tq:(sub + 1) * tq, :] = jnp.transpose(o).astype(o_ref.dtype)


FOX_TQ, FOX_TK, FOX_NSUB = 512, 512, 2


def _fox_prompt(q16, k16, vt16, brep, T):
    M, W = q16.shape
    H = W // FOX_HD
    tq, tk, nsub = FOX_TQ, FOX_TK, FOX_NSUB
    bq = nsub * tq
    assert T % bq == 0 and bq % tk == 0
    return pl.pallas_call(
        functools.partial(_fox_prompt_body, tq=tq, tk=tk, nsub=nsub), grid=(H, T // bq),
        in_specs=[pl.BlockSpec((bq, FOX_HD), lambda h, i: (i, h)),
                  pl.BlockSpec((T, FOX_HD), lambda h, i: (0, h)),
                  pl.BlockSpec((FOX_HD, T), lambda h, i: (h, 0)),
                  pl.BlockSpec((None, T, LANES), lambda h, i: (h, 0, 0))],
        out_specs=pl.BlockSpec((bq, FOX_HD), lambda h, i: (i, h)),
        out_shape=jax.ShapeDtypeStruct((M, W), BF16),
        scratch_shapes=[pltpu.VMEM((nsub, FOX_HD + ONES_ROWS, tq), F32),
                        pltpu.VMEM((2, nsub, tk, tq), F32)],
        compiler_params=_params("parallel", "arbitrary"), name="fox_prompt")(q16, k16, vt16, brep)


def _fox_sample_body(q_ref, kn_ref, vn_ref, kp_ref, vp_ref, c_ref, fo_any, o_ref, *, P, ST, HB):
    del fo_any
    r = lax.broadcasted_iota(jnp.int32, (ST, ST), 0)
    cc = lax.broadcasted_iota(jnp.int32, (ST, ST), 1)
    hd = FOX_HD
    kp = pltpu.einshape("phd->hpd", kp_ref[...]).astype(BF16)
    vp = pltpu.einshape("phd->hpd", vp_ref[...]).astype(BF16)
    scores = []
    for h in range(HB):
        sl = slice(h * hd, (h + 1) * hd)
        q = q_ref[:, sl]
        c = c_ref[h]
        cb = c[:, P - 1:P]
        sp = (lax.dot_general(q, kp[h], _NT, preferred_element_type=F32)
              + (cb - c[:, :P]) * LOG2E)
        sn = (lax.dot_general(q, kn_ref[:, sl], _NT, preferred_element_type=F32)
              + (cb - c[:, P:]) * LOG2E)
        scores.append((sp, jnp.where(cc <= r, sn, -jnp.inf)))
    for h, (sp, sn) in enumerate(scores):
        sl = slice(h * hd, (h + 1) * hd)
        m = jnp.maximum(jnp.max(sp, axis=1, keepdims=True), jnp.max(sn, axis=1, keepdims=True))
        pp = jnp.exp2(sp - m)
        pn = jnp.exp2(sn - m)
        l = jnp.sum(pp, axis=1, keepdims=True) + jnp.sum(pn, axis=1, keepdims=True)
        o = (jnp.dot(pp.astype(BF16), vp[h], preferred_element_type=F32)
             + jnp.dot(pn.astype(BF16), vn_ref[:, sl], preferred_element_type=F32))
        o_ref[:, sl] = (o / l).astype(o_ref.dtype)


def _fox_sample(fo, q16, k16, v16, cache_k, cache_v, ct, layer, T, SB, ST):
    M, W = q16.shape
    H = W // FOX_HD
    P = cache_k.shape[2]
    r0 = T // ST
    HB = SUBLANES
    assert H % HB == 0
    new = pl.BlockSpec((ST, HB * FOX_HD), lambda b, h: (r0 + b, h))
    past = pl.BlockSpec((None, None, P, HB, FOX_HD), lambda b, h: (layer, b, 0, h, 0))
    return pl.pallas_call(
        functools.partial(_fox_sample_body, P=P, ST=ST, HB=HB), grid=(SB, H // HB),
        in_specs=[new, new, new, past, past,
                  pl.BlockSpec((None, HB, 1, P + ST), lambda b, h: (b, h, 0, 0)),
                  pl.BlockSpec(memory_space=pl.ANY)],
        out_specs=new,
        out_shape=jax.ShapeDtypeStruct((M, W), BF16),
        input_output_aliases={6: 0},
        compiler_params=_params("parallel", "arbitrary"), name="fox_sample")(
            q16, k16, v16, cache_k, cache_v, ct, fo)


def _gla_body(q_ref, k_ref, v_ref, r_ref, la_ref, s0_ref, g_ref, go_any, o_ref, sout_ref,
              S_sc, b_sc, q_sc, *, C, NH, DK, DV, nc):
    del go_any
    c = pl.program_id(1)

    @pl.when(c == 0)
    def _():
        S_sc[...] = s0_ref[...]

    b = _row_cumsum(la_ref[...])
    b_sc[...] = b
    q = q_ref[...] * (DK ** -0.5)
    q_sc[...] = q
    bl = b[C - 1:C, :]
    qe = (q * jnp.exp(b)).astype(BF16)
    kdec = (k_ref[...] * jnp.exp(bl - b)).astype(BF16)
    ebl = jnp.exp(bl)
    v16 = v_ref[...].astype(BF16)

    R = GLA_SUB
    nb = C // R
    k = k_ref[...]
    srow_c = lax.broadcasted_iota(jnp.int32, b.shape, 0)
    at_off = [None] * NH
    for i in range(1, nb):
        b_ref_i = b[R * i - 1:R * i, :]
        in_blk = (srow_c >> GLA_SUB_LOG2) == i
        qt = (q * jnp.exp(jnp.where(in_blk, b - b_ref_i, -jnp.inf))).astype(BF16)
        kt = (k * jnp.exp(jnp.where(srow_c < R * i, b_ref_i - b, -jnp.inf))).astype(BF16)
        for h in range(NH):
            sl = slice(h * DK, (h + 1) * DK)
            part = lax.dot_general(kt[:, sl], qt[:, sl], _NT, preferred_element_type=F32)
            at_off[h] = part if at_off[h] is None else at_off[h] + part

    srow = lax.broadcasted_iota(jnp.int32, (R, DK), 0)
    lane = lax.broadcasted_iota(jnp.int32, (R, C), 1)
    at_diag = [[] for _ in range(NH)]
    for i in range(nb):
        blk = slice(R * i, R * (i + 1))
        ats = [jnp.zeros((R, C), F32) for _ in range(NH)]
        for r in range(R):
            t = R * i + r
            mask = srow <= r
            lm = lane == t
            for h in range(NH):
                sl = slice(h * DK, (h + 1) * DK)
                e = jnp.exp(jnp.where(mask, b_sc[t:t + 1, sl] - b_sc[blk, sl], -jnp.inf))
                w = e * (q_sc[t:t + 1, sl] * k_ref[blk, sl])
                ats[h] = jnp.where(lm, jnp.sum(w, axis=1, keepdims=True), ats[h])
        for h in range(NH):
            at_diag[h].append(ats[h])
    ats = []
    for h in range(NH):
        at = jnp.concatenate(at_diag[h], axis=0)
        ats.append(at if at_off[h] is None else at + at_off[h])

    g = g_ref[...]
    for h in range(NH):
        sl = slice(h * DK, (h + 1) * DK)
        vs = slice(h * DV, (h + 1) * DV)
        S = S_sc[h]
        o = (jnp.dot(qe[:, sl], S.astype(BF16), preferred_element_type=F32)
             + lax.dot_general(ats[h].astype(BF16), v16[:, vs], _TN, preferred_element_type=F32))
        r = r_ref[:, vs]
        o = _rms(o) * g * (r * (1.0 / (1.0 + jnp.exp(-r))))
        o_ref[:, vs] = o.astype(o_ref.dtype)
        dm = jnp.transpose(jnp.broadcast_to(ebl[:, sl], (DK, DK)))
        decay = jnp.concatenate([dm] * (DV // DK), axis=1)
        S_sc[h] = decay * S + lax.dot_general(kdec[:, sl], v16[:, vs], _TN, preferred_element_type=F32)

    @pl.when(c == nc - 1)
    def _():
        sout_ref[...] = S_sc[...]


def _gla(rest, la, s0, g, go, *, row0, nseq, nc, C, DK, DV):
    M = rest.shape[0]
    NH = GLA_HEADS
    KW, VW = NH * DK, NH * DV
    r0 = row0 // C
    assert row0 % C == 0 and VW == 2 * KW

    def rows(col):
        return lambda s, c: (r0 + s * nc + c, col)

    in_specs = [pl.BlockSpec((C, KW), rows(0)), pl.BlockSpec((C, KW), rows(1)),
                pl.BlockSpec((C, VW), rows(1)), pl.BlockSpec((C, VW), rows(2)),
                pl.BlockSpec((C, KW), rows(0)),
                pl.BlockSpec((None, NH, DK, DV), lambda s, c: (s, 0, 0, 0)),
                pl.BlockSpec((1, DV), lambda s, c: (0, 0))]
    args = [rest, rest, rest, rest, la, s0, g.reshape(1, DV)]
    aliases = {}
    if go is not None:
        in_specs.append(pl.BlockSpec(memory_space=pl.ANY))
        args.append(go)
        aliases = {7: 0}
        body = _gla_body
    else:
        body = functools.partial(_gla_body_noalias)
    return pl.pallas_call(
        functools.partial(body, C=C, NH=NH, DK=DK, DV=DV, nc=nc), grid=(nseq, nc),
        in_specs=in_specs,
        out_specs=[pl.BlockSpec((C, VW), rows(0)),
                   pl.BlockSpec((None, NH, DK, DV), lambda s, c: (s, 0, 0, 0))],
        out_shape=[jax.ShapeDtypeStruct((M, VW), BF16),
                   jax.ShapeDtypeStruct((nseq, NH, DK, DV), F32)],
        scratch_shapes=[pltpu.VMEM((NH, DK, DV), F32), pltpu.VMEM((C, KW), F32),
                        pltpu.VMEM((C, KW), F32)],
        input_output_aliases=aliases,
        compiler_params=_params("parallel", "arbitrary"), name="gla")(*args)


def _gla_body_noalias(q_ref, k_ref, v_ref, r_ref, la_ref, s0_ref, g_ref, o_ref, sout_ref,
                      S_sc, b_sc, q_sc, **kw):
    _gla_body(q_ref, k_ref, v_ref, r_ref, la_ref, s0_ref, g_ref, None, o_ref, sout_ref,
              S_sc, b_sc, q_sc, **kw)


def _pool_body(*refs, bm, pos0, GC, aliased):
    if aliased:
        p_ref, hist_ref, w_ref, sp_ref, _, o_ref, hout_ref, xa_sc = refs
    else:
        p_ref, hist_ref, w_ref, sp_ref, o_ref, hout_ref, xa_sc = refs
    i = pl.program_id(1)

    @pl.when(i == 0)
    def _():
        xa_sc[0:POOL_CARRY, :] = hist_ref[...]

    xa_sc[POOL_CARRY:POOL_CARRY + bm, :] = p_ref[...]
    pos = pos0 + i * bm + lax.broadcasted_iota(jnp.int32, (bm, 1), 0)
    for gi, w in enumerate(POOL_WINDOWS):
        cs = slice(gi * GC, (gi + 1) * GC)
        x = xa_sc[POOL_CARRY:POOL_CARRY + bm, cs]
        tot = x
        for d in range(1, w):
            tot = tot + xa_sc[POOL_CARRY - d:POOL_CARRY - d + bm, cs]
        cnt = jnp.minimum(pos + 1, w).astype(F32)
        dd = (tot / cnt - x).astype(BF16)
        y = jnp.dot(dd, w_ref[gi], preferred_element_type=F32) * sp_ref[:, cs]
        o_ref[:, cs] = y.astype(o_ref.dtype)
    last = xa_sc[bm:bm + POOL_CARRY, :]
    hout_ref[...] = last
    xa_sc[0:POOL_CARRY, :] = last


def _pool(rest, col, hist, w16, sp, po, *, row0, nseq, nb, bm, pos0):
    M = rest.shape[0]
    G, GC, _ = w16.shape
    PW = G * GC
    r0 = row0 // bm
    assert row0 % bm == 0 and bm >= POOL_CARRY
    in_specs = [pl.BlockSpec((bm, PW), lambda s, i: (r0 + s * nb + i, col)),
                pl.BlockSpec((None, POOL_CARRY, PW), lambda s, i: (s, 0, 0)),
                pl.BlockSpec((G, GC, GC), lambda s, i: (0, 0, 0)),
                pl.BlockSpec((1, PW), lambda s, i: (0, 0))]
    args = [rest, hist, w16, sp.reshape(1, PW)]
    aliases = {}
    if po is not None:
        in_specs.append(pl.BlockSpec(memory_space=pl.ANY))
        args.append(po)
        aliases = {4: 0}
    return pl.pallas_call(
        functools.partial(_pool_body, bm=bm, pos0=pos0, GC=GC, aliased=po is not None),
        grid=(nseq, nb), in_specs=in_specs,
        out_specs=[pl.BlockSpec((bm, PW), lambda s, i: (r0 + s * nb + i, 0)),
                   pl.BlockSpec((None, POOL_CARRY, PW), lambda s, i: (s, 0, 0))],
        out_shape=[jax.ShapeDtypeStruct((M, PW), BF16),
                   jax.ShapeDtypeStruct((nseq, POOL_CARRY, PW), F32)],
        scratch_shapes=[pltpu.VMEM((POOL_CARRY + bm, PW), F32)],
        input_output_aliases=aliases,
        compiler_params=_params("parallel", "arbitrary"), name="pool")(*args)


def kernel(x_prompt, x_sample, cache_fox_k, cache_fox_v, cache_fox_logf, state_gla, state_pool, g_pre_mix, w_in, b_fox_f, w_gla_a2, b_gla_a, g_gla_norm, w_pool, s_pool, w_out, g_post_mix, g_pre_mlp, w_up, w_down, g_post_mlp):
    B, T, D = x_prompt.shape
    SB, ST, _ = x_sample.shape
    L = w_in.shape[0]
    P = cache_fox_k.shape[2]
    assert B == 1
    FW = D // 2
    H = FW // FOX_HD
    GW = D // 4
    DV = GW // GLA_HEADS
    DK = DV // 2
    KW = GLA_HEADS * DK
    RANK = w_gla_a2.shape[1]
    PW = D - FW - GW
    GC = PW // len(POOL_WINDOWS)
    MS = SB * ST
    M = T + MS
    assert H + RANK <= LANES and 2 * KW + 2 * GW == 3 * PW and PW % LANES == 0

    o_fq, o_fk, o_fv, o_ff = 0, FW, 2 * FW, 3 * FW
    o_gq = o_ff + H
    o_ga = o_gq + 2 * KW + GW
    o_gr = o_ga + RANK

    w_in16 = w_in.astype(BF16)
    w_rest16 = jnp.concatenate([w_in16[:, :, o_gq:o_ga], w_in16[:, :, o_gr:]], axis=2)
    w_tail16 = jnp.concatenate([w_in16[:, :, o_ff:o_gq], w_in16[:, :, o_ga:o_gr],
                                jnp.zeros((L, D, LANES - H - RANK), BF16)], axis=2)
    w_out16, w_down16, w_pool16 = (w.astype(BF16) for w in (w_out, w_down, w_pool))
    past_logf = jnp.pad(cache_fox_logf.astype(F32), ((0, 0), (0, 0), (0, 0), (0, LANES - H)))
    zero_state = jnp.zeros((1, GLA_HEADS, DK, DV), F32)
    zero_hist = jnp.zeros((1, POOL_CARRY, PW), F32)
    sample_hist = jnp.pad(state_pool, ((0, 0), (0, 0), (POOL_CARRY - POOL_HIST, 0), (0, 0)))

    h = None
    x_groups = ((x_prompt.reshape(T, D), 0), (x_sample.reshape(MS, D), T))
    a = None
    for xg, row0 in x_groups:
        a = _norm_first(xg, g_pre_mix[0], a, row0, M)
    outs_p = [[] for _ in range(5)]
    outs_s = [[] for _ in range(5)]
    kv_p, kv_s = [None, None], [None, None]
    for l in range(L):
        bfox_pad = jnp.pad(b_fox_f[l].astype(F32), (0, LANES - H)).reshape(1, LANES)
        w2_pad = jnp.pad(w_gla_a2[l].astype(F32), ((H, LANES - H - RANK), (0, 0)))

        (q16,) = _matmul(a, w_in16, l, [BF16], name="proj_q", col0=o_fq, ncols=FW,
                         scale=FOX_HD ** -0.5 * LOG2E)
        kv16 = []
        for t, (name, col0) in enumerate((("proj_k", o_fk), ("proj_v", o_fv))):
            common = dict(name=name, col0=col0, ncols=FW, layer=l, nlayers=L, hd=FOX_HD)
            kv_p[t], x16 = _matmul_heads(a, w_in16, kv_p[t], None, row0=0, nrows=T, **common)
            kv_s[t], x16 = _matmul_heads(a, w_in16, kv_s[t], x16, row0=T, nrows=MS, **common)
            kv16.append(x16)
        k16, v16 = kv16
        (rest,) = _matmul(a, w_rest16, l, [F32], name="proj_rest")
        (tail,) = _matmul(a, w_tail16, l, [F32], name="proj_tail")
        logf, la = _gates(tail, bfox_pad, w2_pad, b_gla_a[l].reshape(1, KW))

        c_p = _cumsum(logf[:T].reshape(1, T, LANES), 1024)
        brep = jnp.broadcast_to(jnp.transpose(c_p[0, :, :H] * (-LOG2E))[:, :, None], (H, T, LANES))
        vt16 = jnp.transpose(v16[:T])
        fo = _fox_prompt(q16, k16, vt16, brep, T)
        lf_all = jnp.concatenate([past_logf[l], logf[T:].reshape(SB, ST, LANES)], axis=1)
        c_s = _cumsum(lf_all, P + ST)
        ct_s = jnp.transpose(c_s[:, :, :H], (0, 2, 1)).reshape(SB, H, 1, P + ST)
        fo = _fox_sample(fo, q16, k16, v16, cache_fox_k, cache_fox_v, ct_s, l, T, SB, ST)

        go, gla_p = _gla(rest, la, zero_state, g_gla_norm[l], None,
                         row0=0, nseq=1, nc=T // GLA_CHUNK, C=GLA_CHUNK, DK=DK, DV=DV)
        go, gla_s = _gla(rest, la, state_gla[l], g_gla_norm[l], go,
                         row0=T, nseq=SB, nc=1, C=ST, DK=DK, DV=DV)

        pcol = (2 * KW + 2 * GW) // PW
        bp = _pick(T, 512)
        po, hist_p = _pool(rest, pcol, zero_hist, w_pool16[l], s_pool[l], None,
                           row0=0, nseq=1, nb=T // bp, bm=bp, pos0=0)
        po, hist_s = _pool(rest, pcol, sample_hist[l], w_pool16[l], s_pool[l], po,
                           row0=T, nseq=SB, nb=1, bm=ST, pos0=P)

        y = _matmul_cat([fo, go, po], w_out16, l, name="proj_out")
        if h is None:
            hc = None
            for xg, row0 in x_groups:
                hc = _norm_mid(xg, y, g_post_mix[l], g_pre_mlp[l], hc, row0)
            h, c = hc
        else:
            h, c = _norm_mid(h, y, g_post_mix[l], g_pre_mlp[l])
        (hid,) = _matmul(c, w_up, l, [BF16], name="mlp_up", relu2=True, bn=512)
        f = _matmul_kacc(hid, w_down16, l, name="mlp_down")
        if l + 1 < L:
            h, a = _norm_mid(h, f, g_post_mlp[l], g_pre_mix[l + 1])
        else:
            y_prompt = _norm_last(h, f, g_post_mlp[l], 0, T)
            y_sample = _norm_last(h, f, g_post_mlp[l], T, MS)

        outs_p[2].append(logf[:T, :H].reshape(1, T, H))
        outs_p[3].append(gla_p)
        outs_p[4].append(hist_p[:, POOL_CARRY - POOL_HIST:])
        outs_s[2].append(logf[T:, :H].reshape(SB, ST, H))
        outs_s[3].append(gla_s)
        outs_s[4].append(hist_s[:, POOL_CARRY - POOL_HIST:])

    p_kv = [x.reshape(L, 1, T, H, FOX_HD) for x in kv_p]
    s_kv = [x.reshape(L, SB, ST, H, FOX_HD) for x in kv_s]
    return (y_prompt.reshape(1, T, D), y_sample.reshape(SB, ST, D),
            *p_kv, *[jnp.stack(o) for o in outs_p[2:]], *s_kv, *[jnp.stack(o) for o in outs_s[2:]])
```

```python
import functools
import math

import jax
import jax.numpy as jnp
from jax import lax
from jax.experimental import pallas as pl
from jax.experimental.pallas import tpu as pltpu

F32 = jnp.float32
BF16 = jnp.bfloat16

RMS_EPS = 1e-6
FOX_HD = 128
GLA_HEADS = 4
GLA_TAU = 16.0
GLA_CHUNK = 64
GLA_SUB_LOG2 = 4
GLA_SUB = 1 << GLA_SUB_LOG2
POOL_WINDOWS = (2, 4, 8, 16)
POOL_HIST = 15
POOL_CARRY = 16
LANES = 128
SUBLANES = 8
ONES_ROWS = 16
LOG2E = 1.4426950408889634
BIAS_TERMS = 3
VMEM_LIMIT = 56 * 1024 * 1024


def _params(*sem):
    return pltpu.CompilerParams(dimension_semantics=sem, vmem_limit_bytes=VMEM_LIMIT)


def _pick(n, pref, mult=8):
    b = min(pref, n)
    b -= b % mult
    while b > 0 and n % b:
        b -= mult
    assert b > 0, (n, pref, mult)
    return b


def _rms(x):
    return x * lax.rsqrt(jnp.mean(x * x, axis=-1, keepdims=True) + RMS_EPS)


def _log_sigmoid(x):
    return jnp.minimum(x, 0.0) - jnp.log1p(jnp.exp(-jnp.abs(x)))


def _norm_first_body(h_ref, g_ref, a_ref):
    a_ref[...] = (_rms(h_ref[...]) * g_ref[...]).astype(a_ref.dtype)


def _norm_mid_body(h_ref, y_ref, gy_ref, g_ref, h_out, a_out):
    h = h_ref[...] + _rms(y_ref[...]) * gy_ref[...]
    h_out[...] = h
    a_out[...] = (_rms(h) * g_ref[...]).astype(a_out.dtype)


def _norm_last_body(h_ref, y_ref, gy_ref, h_out):
    h_out[...] = h_ref[...] + _rms(y_ref[...]) * gy_ref[...]


def _row_block(row0, nrows, pref):
    return _pick(math.gcd(row0, nrows) if row0 else nrows, pref)


def _norm_first(x, g, prev, row0, M):
    nrows, D = x.shape
    bm = _row_block(row0, nrows, 256)
    r0 = row0 // bm
    in_specs = [pl.BlockSpec((bm, D), lambda i: (i, 0)), pl.BlockSpec((1, D), lambda i: (0, 0))]
    args = [x, g.reshape(1, D)]
    if prev is not None:
        in_specs.append(pl.BlockSpec(memory_space=pl.ANY))
        args.append(prev)
    return pl.pallas_call(
        lambda x_ref, g_ref, *rest: _norm_first_body(x_ref, g_ref, rest[-1]),
        grid=(nrows // bm,), in_specs=in_specs, out_specs=pl.BlockSpec((bm, D), lambda i: (r0 + i, 0)),
        out_shape=jax.ShapeDtypeStruct((M, D), BF16),
        input_output_aliases={2: 0} if prev is not None else {},
        compiler_params=_params("parallel"), name="norm_first")(*args)


def _norm_mid(h_src, y, gy, g, prev=None, row0=0):
    nrows, D = h_src.shape
    M = y.shape[0]
    bm = _row_block(row0, nrows, 256)
    r0 = row0 // bm
    src = pl.BlockSpec((bm, D), lambda i: (i, 0))
    row = pl.BlockSpec((bm, D), lambda i: (r0 + i, 0))
    vec = pl.BlockSpec((1, D), lambda i: (0, 0))
    in_specs = [src, row, vec, vec]
    args = [h_src, y, gy.reshape(1, D), g.reshape(1, D)]
    if prev is not None:
        in_specs += [pl.BlockSpec(memory_space=pl.ANY)] * 2
        args += list(prev)
    return pl.pallas_call(
        lambda h_ref, y_ref, gy_ref, g_ref, *rest: _norm_mid_body(h_ref, y_ref, gy_ref, g_ref, *rest[-2:]),
        grid=(nrows // bm,), in_specs=in_specs, out_specs=[row, row],
        out_shape=[jax.ShapeDtypeStruct((M, D), F32), jax.ShapeDtypeStruct((M, D), BF16)],
        input_output_aliases={4: 0, 5: 1} if prev is not None else {},
        compiler_params=_params("parallel"), name="norm_mid")(*args)


def _norm_last(h, y, gy, row0, nrows):
    D = h.shape[1]
    bm = _pick(math.gcd(row0, nrows) if row0 else nrows, 256)
    r0 = row0 // bm
    row = pl.BlockSpec((bm, D), lambda i: (r0 + i, 0))
    vec = pl.BlockSpec((1, D), lambda i: (0, 0))
    return pl.pallas_call(
        _norm_last_body, grid=(nrows // bm,), in_specs=[row, row, vec],
        out_specs=pl.BlockSpec((bm, D), lambda i: (i, 0)),
        out_shape=jax.ShapeDtypeStruct((nrows, D), F32),
        compiler_params=_params("parallel"), name="norm_last")(h, y, gy.reshape(1, D))


def _mm_body(x_ref, w_ref, *o_refs, relu2, scale):
    acc = jnp.dot(x_ref[...], w_ref[...], preferred_element_type=F32)
    if relu2:
        acc = jnp.square(jnp.maximum(acc, 0.0))
    if scale is not None:
        acc = acc * scale
    for o in o_refs:
        o[...] = acc.astype(o.dtype)


def _matmul(x, w, layer, out_dtypes, *, name, relu2=False, scale=None, col0=0, ncols=None, bm=1024, bn=1024):
    M, K = x.shape
    N = w.shape[2] - col0 if ncols is None else ncols
    bm = _pick(M, bm)
    bn = _pick(N, bn, LANES)
    assert col0 % bn == 0
    c0 = col0 // bn
    outs = pl.pallas_call(
        functools.partial(_mm_body, relu2=relu2, scale=scale), grid=(M // bm, N // bn),
        in_specs=[pl.BlockSpec((bm, K), lambda i, j: (i, 0)),
                  pl.BlockSpec((None, K, bn), lambda i, j: (layer, 0, c0 + j))],
        out_specs=[pl.BlockSpec((bm, bn), lambda i, j: (i, j)) for _ in out_dtypes],
        out_shape=[jax.ShapeDtypeStruct((M, N), dt) for dt in out_dtypes],
        compiler_params=_params("parallel", "arbitrary"), name=name)(x, w)
    return outs


def _mm_heads_body(x_ref, w_ref, *refs, hd):
    o32_ref, o16_ref = refs[-2:]
    acc = jnp.dot(x_ref[...], w_ref[...], preferred_element_type=F32)
    o16_ref[...] = acc.astype(o16_ref.dtype)
    o32_ref[...] = pltpu.einshape("m(hd)->mhd", acc, d=hd)


def _matmul_heads(x, w, prev32, prev16, *, name, col0, ncols, row0, nrows, layer, nlayers, hd, bm=512, bn=1024):
    M, K = x.shape
    bm = _pick(math.gcd(row0, nrows) if row0 else nrows, bm)
    bn = _pick(ncols, bn, LANES * SUBLANES)
    assert col0 % bn == 0 and bn % hd == 0
    c0, r0, hb = col0 // bn, row0 // bm, bn // hd
    in_specs = [pl.BlockSpec((bm, K), lambda i, j: (r0 + i, 0)),
                pl.BlockSpec((None, K, bn), lambda i, j: (layer, 0, c0 + j))]
    args = [x, w]
    aliases = {}
    for out_idx, prev in enumerate((prev32, prev16)):
        if prev is not None:
            aliases[len(args)] = out_idx
            in_specs.append(pl.BlockSpec(memory_space=pl.ANY))
            args.append(prev)
    return pl.pallas_call(
        functools.partial(_mm_heads_body, hd=hd), grid=(nrows // bm, ncols // bn), in_specs=in_specs,
        out_specs=[pl.BlockSpec((None, bm, hb, hd), lambda i, j: (layer, i, j, 0)),
                   pl.BlockSpec((bm, bn), lambda i, j: (r0 + i, j))],
        out_shape=[jax.ShapeDtypeStruct((nlayers, nrows, ncols // hd, hd), F32),
                   jax.ShapeDtypeStruct((M, ncols), BF16)],
        input_output_aliases=aliases,
        compiler_params=_params("parallel", "arbitrary"), name=name)(*args)


def _mm_cat_body(*refs, splits):
    x_refs, w_ref, o_ref = refs[:-2], refs[-2], refs[-1]
    acc = None
    off = 0
    for x_ref, kk in zip(x_refs, splits):
        part = jnp.dot(x_ref[...], w_ref[off:off + kk, :], preferred_element_type=F32)
        acc = part if acc is None else acc + part
        off += kk
    o_ref[...] = acc


def _matmul_cat(xs, w, layer, *, name, bm=1024, bn=1024):
    M = xs[0].shape[0]
    _, K, N = w.shape
    splits = tuple(x.shape[1] for x in xs)
    assert sum(splits) == K
    bm = _pick(M, bm)
    bn = _pick(N, bn, LANES)
    in_specs = [pl.BlockSpec((bm, kk), lambda i, j: (i, 0)) for kk in splits]
    in_specs.append(pl.BlockSpec((None, K, bn), lambda i, j: (layer, 0, j)))
    return pl.pallas_call(
        functools.partial(_mm_cat_body, splits=splits), grid=(M // bm, N // bn),
        in_specs=in_specs, out_specs=pl.BlockSpec((bm, bn), lambda i, j: (i, j)),
        out_shape=jax.ShapeDtypeStruct((M, N), F32),
        compiler_params=_params("parallel", "arbitrary"), name=name)(*xs, w)


def _mm_kacc_body(x_ref, w_ref, o_ref, *, bn):
    k = pl.program_id(1)

    @pl.when(k == 0)
    def _():
        o_ref[...] = jnp.zeros(o_ref.shape, F32)

    x = x_ref[...]
    for n0 in range(0, o_ref.shape[1], bn):
        o_ref[:, n0:n0 + bn] += jnp.dot(x, w_ref[:, n0:n0 + bn], preferred_element_type=F32)


def _matmul_kacc(x, w, layer, *, name, bm=512, bk=1024):
    M, K = x.shape
    N = w.shape[2]
    bm = _pick(M, bm)
    bk = _pick(K, bk, LANES)
    return pl.pallas_call(
        functools.partial(_mm_kacc_body, bn=_pick(N, 512, LANES)), grid=(M // bm, K // bk),
        in_specs=[pl.BlockSpec((bm, bk), lambda i, k: (i, k)),
                  pl.BlockSpec((None, bk, N), lambda i, k: (layer, k, 0))],
        out_specs=pl.BlockSpec((bm, N), lambda i, k: (i, 0)),
        out_shape=jax.ShapeDtypeStruct((M, N), F32),
        compiler_params=_params("parallel", "arbitrary"), name=name)(x, w)


def _gates_body(t_ref, bf_ref, w2_ref, bg_ref, logf_ref, la_ref):
    t = t_ref[...]
    logf_ref[...] = _log_sigmoid(t + bf_ref[...])
    zg = jnp.dot(t, w2_ref[...], preferred_element_type=F32) + bg_ref[...]
    la_ref[...] = _log_sigmoid(zg) * (1.0 / GLA_TAU)


def _gates(tail, bfox_pad, w2_pad, bgla):
    M = tail.shape[0]
    KW = w2_pad.shape[1]
    bm = _pick(M, 512)
    return pl.pallas_call(
        _gates_body, grid=(M // bm,),
        in_specs=[pl.BlockSpec((bm, LANES), lambda i: (i, 0)),
                  pl.BlockSpec((1, LANES), lambda i: (0, 0)),
                  pl.BlockSpec((LANES, KW), lambda i: (0, 0)),
                  pl.BlockSpec((1, KW), lambda i: (0, 0))],
        out_specs=[pl.BlockSpec((bm, LANES), lambda i: (i, 0)),
                   pl.BlockSpec((bm, KW), lambda i: (i, 0))],
        out_shape=[jax.ShapeDtypeStruct((M, LANES), F32), jax.ShapeDtypeStruct((M, KW), F32)],
        compiler_params=_params("parallel"), name="gates")(tail, bfox_pad, w2_pad, bgla)


def _row_cumsum(x):
    n = x.shape[0]
    row = lax.broadcasted_iota(jnp.int32, x.shape, 0)
    sh = 1
    while sh < n:
        x = x + jnp.where(row >= sh, pltpu.roll(x, sh, axis=0), 0.0)
        sh *= 2
    return x


def _cumsum_body(x_ref, c_ref, carry_sc):
    @pl.when(pl.program_id(1) == 0)
    def _():
        carry_sc[...] = jnp.zeros(carry_sc.shape, F32)

    c = _row_cumsum(x_ref[...]) + carry_sc[...]
    c_ref[...] = c
    n = c.shape[0]
    carry_sc[...] = c[n - 1:n, :]


def _cumsum(x, bl):
    S, L, W = x.shape
    bl = _pick(L, bl)
    spec = pl.BlockSpec((None, bl, W), lambda s, i: (s, i, 0))
    return pl.pallas_call(
        _cumsum_body, grid=(S, L // bl), in_specs=[spec], out_specs=spec,
        out_shape=jax.ShapeDtypeStruct((S, L, W), F32),
        scratch_shapes=[pltpu.VMEM((1, W), F32)],
        compiler_params=_params("parallel", "arbitrary"), name="cumsum")(x)


_NT = (((1,), (1,)), ((), ()))
_TN = (((0,), (0,)), ((), ()))


def _fox_prompt_body(q_ref, k_ref, vt_ref, o_ref, acc_sc, st_sc, qa_sc, *, tq, tk, nsub):
    qi = pl.program_id(1)
    hd = vt_ref.shape[0]
    bq = nsub * tq
    nkb = bq // tk
    ones = jnp.ones((ONES_ROWS, tk), BF16)
    lane = lax.broadcasted_iota(jnp.int32, (bq, hd), 1)
    qa_sc[:, :hd] = q_ref[...]
    qa_sc[:, hd:] = jnp.where(lane < BIAS_TERMS, 1.0, 0.0).astype(BF16)

    def mask(st, mask_off):
        key = lax.broadcasted_iota(jnp.int32, (tk, tq), 0) + mask_off
        qry = lax.broadcasted_iota(jnp.int32, (tk, tq), 1)
        return jnp.where(key <= qry, st, -jnp.inf)

    def scores(j, sub):
        s0 = pl.multiple_of(j * tk, tk)
        return lax.dot_general(k_ref[pl.ds(s0, tk), :], qa_sc[sub * tq:(sub + 1) * tq, :], _NT,
                               preferred_element_type=F32)

    def accumulate(j, sub, st, m_blk, m_prev):
        s0 = pl.multiple_of(j * tk, tk)
        m_new = jnp.maximum(m_prev, m_blk)
        alpha = jnp.exp2(m_prev - m_new)
        pt = jnp.exp2(st - m_new).astype(BF16)
        vt = jnp.concatenate([vt_ref[:, pl.ds(s0, tk)], ones], axis=0)
        acc_sc[sub] = alpha * acc_sc[sub] + jnp.dot(vt, pt, preferred_element_type=F32)
        return m_new

    def issue(j, slot):
        mb = []
        for sub in range(nsub):
            st = scores(j, sub)
            st_sc[slot, sub] = st
            mb.append(jnp.max(st, axis=0, keepdims=True))
        return tuple(mb)

    acc_sc[...] = jnp.zeros(acc_sc.shape, F32)
    m0 = tuple(jnp.full((1, tq), -jnp.inf, F32) for _ in range(nsub))
    n = nkb * qi

    def body(jj, carry):
        ms, mb = carry
        for u in range(nkb):
            slot = u % 2
            j = nkb * jj + u
            mb_next = issue(j + 1, 1 - slot)
            ms = tuple(accumulate(j, sub, st_sc[slot, sub], mb[sub], ms[sub]) for sub in range(nsub))
            mb = mb_next
        return ms, mb

    assert nkb % 2 == 0
    ms, _ = lax.fori_loop(0, qi, body, (m0, issue(0, 0)))
    ms = list(ms)
    band = [(d, sub) for d in range(nkb) for sub in range(nsub) if d * tk <= (sub + 1) * tq - 1]
    late = {(d, sub): scores(n + d, sub) for d, sub in band if d > 0}
    for d, sub in band:
        k_lo, k_hi, q_lo = d * tk, (d + 1) * tk - 1, sub * tq
        st = st_sc[0, sub] if d == 0 else late[d, sub]
        if k_hi > q_lo:
            st = mask(st, k_lo - q_lo)
        ms[sub] = accumulate(n + d, sub, st, jnp.max(st, axis=0, keepdims=True), ms[sub])
    for sub in range(nsub):
        acc = acc_sc[sub]
        o = acc[:hd, :] / acc[hd:hd + 1, :]
        o_ref[sub * tq:(sub + 1) * tq, :] = jnp.transpose(o).astype(o_ref.dtype)


FOX_TQ, FOX_TK, FOX_NSUB = 512, 512, 4


def _fox_prompt(q16, kb16, vt16, T):
    M, W = q16.shape
    H = W // FOX_HD
    tq, tk, nsub = FOX_TQ, FOX_TK, FOX_NSUB
    bq = nsub * tq
    assert T % bq == 0 and bq % tk == 0
    return pl.pallas_call(
        functools.partial(_fox_prompt_body, tq=tq, tk=tk, nsub=nsub), grid=(H, T // bq),
        in_specs=[pl.BlockSpec((bq, FOX_HD), lambda h, i: (i, h)),
                  pl.BlockSpec((T, 2 * FOX_HD), lambda h, i: (0, h)),
                  pl.BlockSpec((FOX_HD, T), lambda h, i: (h, 0))],
        out_specs=pl.BlockSpec((bq, FOX_HD), lambda h, i: (i, h)),
        out_shape=jax.ShapeDtypeStruct((M, W), BF16),
        scratch_shapes=[pltpu.VMEM((nsub, FOX_HD + ONES_ROWS, tq), F32),
                        pltpu.VMEM((2, nsub, tk, tq), F32),
                        pltpu.VMEM((bq, 2 * FOX_HD), BF16)],
        compiler_params=_params("parallel", "arbitrary"), name="fox_prompt")(q16, kb16, vt16)


def _split_bf16(x, terms):
    parts = []
    for _ in range(terms):
        top = lax.bitcast_convert_type(lax.bitcast_convert_type(x, jnp.uint32) & jnp.uint32(0xFFFF0000), F32)
        parts.append(top.astype(BF16))
        x = x - top
    return jnp.stack(parts, axis=-1)


def _fox_sample_body(q_ref, kn_ref, vn_ref, kp_ref, vp_ref, c_ref, fo_any, o_ref, *, P, ST, HB):
    del fo_any
    r = lax.broadcasted_iota(jnp.int32, (ST, ST), 0)
    cc = lax.broadcasted_iota(jnp.int32, (ST, ST), 1)
    hd = FOX_HD
    kp = pltpu.einshape("phd->hpd", kp_ref[...]).astype(BF16)
    vp = pltpu.einshape("phd->hpd", vp_ref[...]).astype(BF16)
    scores = []
    for h in range(HB):
        sl = slice(h * hd, (h + 1) * hd)
        q = q_ref[:, sl]
        c = c_ref[h]
        cb = c[:, P - 1:P]
        sp = (lax.dot_general(q, kp[h], _NT, preferred_element_type=F32)
              + (cb - c[:, :P]) * LOG2E)
        sn = (lax.dot_general(q, kn_ref[:, sl], _NT, preferred_element_type=F32)
              + (cb - c[:, P:]) * LOG2E)
        scores.append((sp, jnp.where(cc <= r, sn, -jnp.inf)))
    for h, (sp, sn) in enumerate(scores):
        sl = slice(h * hd, (h + 1) * hd)
        m = jnp.maximum(jnp.max(sp, axis=1, keepdims=True), jnp.max(sn, axis=1, keepdims=True))
        pp = jnp.exp2(sp - m)
        pn = jnp.exp2(sn - m)
        l = jnp.sum(pp, axis=1, keepdims=True) + jnp.sum(pn, axis=1, keepdims=True)
        o = (jnp.dot(pp.astype(BF16), vp[h], preferred_element_type=F32)
             + jnp.dot(pn.astype(BF16), vn_ref[:, sl], preferred_element_type=F32))
        o_ref[:, sl] = (o / l).astype(o_ref.dtype)


def _fox_sample(fo, q16, k16, v16, cache_k, cache_v, ct, layer, T, SB, ST):
    M, W = q16.shape
    H = W // FOX_HD
    P = cache_k.shape[2]
    r0 = T // ST
    HB = SUBLANES
    assert H % HB == 0
    new = pl.BlockSpec((ST, HB * FOX_HD), lambda b, h: (r0 + b, h))
    past = pl.BlockSpec((None, None, P, HB, FOX_HD), lambda b, h: (layer, b, 0, h, 0))
    return pl.pallas_call(
        functools.partial(_fox_sample_body, P=P, ST=ST, HB=HB), grid=(SB, H // HB),
        in_specs=[new, new, new, past, past,
                  pl.BlockSpec((None, HB, 1, P + ST), lambda b, h: (b, h, 0, 0)),
                  pl.BlockSpec(memory_space=pl.ANY)],
        out_specs=new,
        out_shape=jax.ShapeDtypeStruct((M, W), BF16),
        input_output_aliases={6: 0},
        compiler_params=_params("parallel", "arbitrary"), name="fox_sample")(
            q16, k16, v16, cache_k, cache_v, ct, fo)


def _gla_body(q_ref, k_ref, v_ref, r_ref, la_ref, s0_ref, g_ref, go_any, o_ref, sout_ref,
              S_sc, b_sc, q_sc, *, C, NH, DK, DV, nc):
    del go_any
    c = pl.program_id(1)

    @pl.when(c == 0)
    def _():
        S_sc[...] = s0_ref[...]

    b = _row_cumsum(la_ref[...])
    b_sc[...] = b
    q = q_ref[...] * (DK ** -0.5)
    q_sc[...] = q
    bl = b[C - 1:C, :]
    qe = (q * jnp.exp(b)).astype(BF16)
    kdec = (k_ref[...] * jnp.exp(bl - b)).astype(BF16)
    ebl = jnp.exp(bl)
    v16 = v_ref[...].astype(BF16)

    R = GLA_SUB
    nb = C // R
    k = k_ref[...]
    srow_c = lax.broadcasted_iota(jnp.int32, b.shape, 0)
    at_off = [None] * NH
    for i in range(1, nb):
        b_ref_i = b[R * i - 1:R * i, :]
        in_blk = (srow_c >> GLA_SUB_LOG2) == i
        qt = (q * jnp.exp(jnp.where(in_blk, b - b_ref_i, -jnp.inf))).astype(BF16)
        kt = (k * jnp.exp(jnp.where(srow_c < R * i, b_ref_i - b, -jnp.inf))).astype(BF16)
        for h in range(NH):
            sl = slice(h * DK, (h + 1) * DK)
            part = lax.dot_general(kt[:, sl], qt[:, sl], _NT, preferred_element_type=F32)
            at_off[h] = part if at_off[h] is None else at_off[h] + part

    srow = lax.broadcasted_iota(jnp.int32, (R, DK), 0)
    lane = lax.broadcasted_iota(jnp.int32, (R, C), 1)
    at_diag = [[] for _ in range(NH)]
    for i in range(nb):
        blk = slice(R * i, R * (i + 1))
        ats = [jnp.zeros((R, C), F32) for _ in range(NH)]
        for r in range(R):
            t = R * i + r
            mask = srow <= r
            lm = lane == t
            for h in range(NH):
                sl = slice(h * DK, (h + 1) * DK)
                e = jnp.exp(jnp.where(mask, b_sc[t:t + 1, sl] - b_sc[blk, sl], -jnp.inf))
                w = e * (q_sc[t:t + 1, sl] * k_ref[blk, sl])
                ats[h] = jnp.where(lm, jnp.sum(w, axis=1, keepdims=True), ats[h])
        for h in range(NH):
            at_diag[h].append(ats[h])
    ats = []
    for h in range(NH):
        at = jnp.concatenate(at_diag[h], axis=0)
        ats.append(at if at_off[h] is None else at + at_off[h])

    g = g_ref[...]
    for h in range(NH):
        sl = slice(h * DK, (h + 1) * DK)
        vs = slice(h * DV, (h + 1) * DV)
        S = S_sc[h]
        o = (jnp.dot(qe[:, sl], S.astype(BF16), preferred_element_type=F32)
             + lax.dot_general(ats[h].astype(BF16), v16[:, vs], _TN, preferred_element_type=F32))
        r = r_ref[:, vs]
        o = _rms(o) * g * (r * (1.0 / (1.0 + jnp.exp(-r))))
        o_ref[:, vs] = o.astype(o_ref.dtype)
        dm = jnp.transpose(jnp.broadcast_to(ebl[:, sl], (DK, DK)))
        decay = jnp.concatenate([dm] * (DV // DK), axis=1)
        S_sc[h] = decay * S + lax.dot_general(kdec[:, sl], v16[:, vs], _TN, preferred_element_type=F32)

    @pl.when(c == nc - 1)
    def _():
        sout_ref[...] = S_sc[...]


def _gla(rest, la, s0, g, go, *, row0, nseq, nc, C, DK, DV):
    M = rest.shape[0]
    NH = GLA_HEADS
    KW, VW = NH * DK, NH * DV
    r0 = row0 // C
    assert row0 % C == 0 and VW == 2 * KW

    def rows(col):
        return lambda s, c: (r0 + s * nc + c, col)

    in_specs = [pl.BlockSpec((C, KW), rows(0)), pl.BlockSpec((C, KW), rows(1)),
                pl.BlockSpec((C, VW), rows(1)), pl.BlockSpec((C, VW), rows(2)),
                pl.BlockSpec((C, KW), rows(0)),
                pl.BlockSpec((None, NH, DK, DV), lambda s, c: (s, 0, 0, 0)),
                pl.BlockSpec((1, DV), lambda s, c: (0, 0))]
    args = [rest, rest, rest, rest, la, s0, g.reshape(1, DV)]
    aliases = {}
    if go is not None:
        in_specs.append(pl.BlockSpec(memory_space=pl.ANY))
        args.append(go)
        aliases = {7: 0}
        body = _gla_body
    else:
        body = functools.partial(_gla_body_noalias)
    return pl.pallas_call(
        functools.partial(body, C=C, NH=NH, DK=DK, DV=DV, nc=nc), grid=(nseq, nc),
        in_specs=in_specs,
        out_specs=[pl.BlockSpec((C, VW), rows(0)),
                   pl.BlockSpec((None, NH, DK, DV), lambda s, c: (s, 0, 0, 0))],
        out_shape=[jax.ShapeDtypeStruct((M, VW), BF16),
                   jax.ShapeDtypeStruct((nseq, NH, DK, DV), F32)],
        scratch_shapes=[pltpu.VMEM((NH, DK, DV), F32), pltpu.VMEM((C, KW), F32),
                        pltpu.VMEM((C, KW), F32)],
        input_output_aliases=aliases,
        compiler_params=_params("parallel", "arbitrary"), name="gla")(*args)


def _gla_body_noalias(q_ref, k_ref, v_ref, r_ref, la_ref, s0_ref, g_ref, o_ref, sout_ref,
                      S_sc, b_sc, q_sc, **kw):
    _gla_body(q_ref, k_ref, v_ref, r_ref, la_ref, s0_ref, g_ref, None, o_ref, sout_ref,
              S_sc, b_sc, q_sc, **kw)


def _pool_body(*refs, bm, pos0, GC, aliased):
    if aliased:
        p_ref, hist_ref, w_ref, sp_ref, _, o_ref, hout_ref, xa_sc = refs
    else:
        p_ref, hist_ref, w_ref, sp_ref, o_ref, hout_ref, xa_sc = refs
    i = pl.program_id(1)

    @pl.when(i == 0)
    def _():
        xa_sc[0:POOL_CARRY, :] = hist_ref[...]

    xa_sc[POOL_CARRY:POOL_CARRY + bm, :] = p_ref[...]
    pos = pos0 + i * bm + lax.broadcasted_iota(jnp.int32, (bm, 1), 0)
    for gi, w in enumerate(POOL_WINDOWS):
        cs = slice(gi * GC, (gi + 1) * GC)
        x = xa_sc[POOL_CARRY:POOL_CARRY + bm, cs]
        tot = x
        for d in range(1, w):
            tot = tot + xa_sc[POOL_CARRY - d:POOL_CARRY - d + bm, cs]
        cnt = jnp.minimum(pos + 1, w).astype(F32)
        dd = (tot / cnt - x).astype(BF16)
        y = jnp.dot(dd, w_ref[gi], preferred_element_type=F32) * sp_ref[:, cs]
        o_ref[:, cs] = y.astype(o_ref.dtype)
    last = xa_sc[bm:bm + POOL_CARRY, :]
    hout_ref[...] = last
    xa_sc[0:POOL_CARRY, :] = last


def _pool(rest, col, hist, w16, sp, po, *, row0, nseq, nb, bm, pos0):
    M = rest.shape[0]
    G, GC, _ = w16.shape
    PW = G * GC
    r0 = row0 // bm
    assert row0 % bm == 0 and bm >= POOL_CARRY
    in_specs = [pl.BlockSpec((bm, PW), lambda s, i: (r0 + s * nb + i, col)),
                pl.BlockSpec((None, POOL_CARRY, PW), lambda s, i: (s, 0, 0)),
                pl.BlockSpec((G, GC, GC), lambda s, i: (0, 0, 0)),
                pl.BlockSpec((1, PW), lambda s, i: (0, 0))]
    args = [rest, hist, w16, sp.reshape(1, PW)]
    aliases = {}
    if po is not None:
        in_specs.append(pl.BlockSpec(memory_space=pl.ANY))
        args.append(po)
        aliases = {4: 0}
    return pl.pallas_call(
        functools.partial(_pool_body, bm=bm, pos0=pos0, GC=GC, aliased=po is not None),
        grid=(nseq, nb), in_specs=in_specs,
        out_specs=[pl.BlockSpec((bm, PW), lambda s, i: (r0 + s * nb + i, 0)),
                   pl.BlockSpec((None, POOL_CARRY, PW), lambda s, i: (s, 0, 0))],
        out_shape=[jax.ShapeDtypeStruct((M, PW), BF16),
                   jax.ShapeDtypeStruct((nseq, POOL_CARRY, PW), F32)],
        scratch_shapes=[pltpu.VMEM((POOL_CARRY + bm, PW), F32)],
        input_output_aliases=aliases,
        compiler_params=_params("parallel", "arbitrary"), name="pool")(*args)


def kernel(x_prompt, x_sample, cache_fox_k, cache_fox_v, cache_fox_logf, state_gla, state_pool, g_pre_mix, w_in, b_fox_f, w_gla_a2, b_gla_a, g_gla_norm, w_pool, s_pool, w_out, g_post_mix, g_pre_mlp, w_up, w_down, g_post_mlp):
    B, T, D = x_prompt.shape
    SB, ST, _ = x_sample.shape
    L = w_in.shape[0]
    P = cache_fox_k.shape[2]
    assert B == 1
    FW = D // 2
    H = FW // FOX_HD
    GW = D // 4
    DV = GW // GLA_HEADS
    DK = DV // 2
    KW = GLA_HEADS * DK
    RANK = w_gla_a2.shape[1]
    PW = D - FW - GW
    GC = PW // len(POOL_WINDOWS)
    MS = SB * ST
    M = T + MS
    assert H + RANK <= LANES and 2 * KW + 2 * GW == 3 * PW and PW % LANES == 0

    o_fq, o_fk, o_fv, o_ff = 0, FW, 2 * FW, 3 * FW
    o_gq = o_ff + H
    o_ga = o_gq + 2 * KW + GW
    o_gr = o_ga + RANK

    w_in16 = w_in.astype(BF16)
    w_rest16 = jnp.concatenate([w_in16[:, :, o_gq:o_ga], w_in16[:, :, o_gr:]], axis=2)
    w_tail16 = jnp.concatenate([w_in16[:, :, o_ff:o_gq], w_in16[:, :, o_ga:o_gr],
                                jnp.zeros((L, D, LANES - H - RANK), BF16)], axis=2)
    w_out16, w_up16, w_down16, w_pool16 = (w.astype(BF16) for w in (w_out, w_up, w_down, w_pool))
    past_logf = jnp.pad(cache_fox_logf.astype(F32), ((0, 0), (0, 0), (0, 0), (0, LANES - H)))
    zero_state = jnp.zeros((1, GLA_HEADS, DK, DV), F32)
    zero_hist = jnp.zeros((1, POOL_CARRY, PW), F32)
    sample_hist = jnp.pad(state_pool, ((0, 0), (0, 0), (POOL_CARRY - POOL_HIST, 0), (0, 0)))

    h = None
    x_groups = ((x_prompt.reshape(T, D), 0), (x_sample.reshape(MS, D), T))
    a = None
    for xg, row0 in x_groups:
        a = _norm_first(xg, g_pre_mix[0], a, row0, M)
    outs_p = [[] for _ in range(5)]
    outs_s = [[] for _ in range(5)]
    kv_p, kv_s = [None, None], [None, None]
    for l in range(L):
        bfox_pad = jnp.pad(b_fox_f[l].astype(F32), (0, LANES - H)).reshape(1, LANES)
        w2_pad = jnp.pad(w_gla_a2[l].astype(F32), ((H, LANES - H - RANK), (0, 0)))

        (q16,) = _matmul(a, w_in16, l, [BF16], name="proj_q", col0=o_fq, ncols=FW,
                         scale=FOX_HD ** -0.5 * LOG2E)
        kv16 = []
        for t, (name, col0) in enumerate((("proj_k", o_fk), ("proj_v", o_fv))):
            common = dict(name=name, col0=col0, ncols=FW, layer=l, nlayers=L, hd=FOX_HD)
            kv_p[t], x16 = _matmul_heads(a, w_in16, kv_p[t], None, row0=0, nrows=T, **common)
            kv_s[t], x16 = _matmul_heads(a, w_in16, kv_s[t], x16, row0=T, nrows=MS, **common)
            kv16.append(x16)
        k16, v16 = kv16
        (rest,) = _matmul(a, w_rest16, l, [F32], name="proj_rest")
        (tail,) = _matmul(a, w_tail16, l, [F32], name="proj_tail")
        logf, la = _gates(tail, bfox_pad, w2_pad, b_gla_a[l].reshape(1, KW))

        c_p = _cumsum(logf[:T].reshape(1, T, LANES), 1024)
        bias = _split_bf16(c_p[0, :, :H] * (-LOG2E), BIAS_TERMS)
        kb16 = jnp.concatenate([k16[:T].reshape(T, H, FOX_HD), bias,
                                jnp.zeros((T, H, FOX_HD - BIAS_TERMS), BF16)], axis=-1).reshape(T, 2 * FW)
        vt16 = jnp.transpose(v16[:T])
        fo = _fox_prompt(q16, kb16, vt16, T)
        lf_all = jnp.concatenate([past_logf[l], logf[T:].reshape(SB, ST, LANES)], axis=1)
        c_s = _cumsum(lf_all, P + ST)
        ct_s = jnp.transpose(c_s[:, :, :H], (0, 2, 1)).reshape(SB, H, 1, P + ST)
        fo = _fox_sample(fo, q16, k16, v16, cache_fox_k, cache_fox_v, ct_s, l, T, SB, ST)

        go, gla_p = _gla(rest, la, zero_state, g_gla_norm[l], None,
                         row0=0, nseq=1, nc=T // GLA_CHUNK, C=GLA_CHUNK, DK=DK, DV=DV)
        go, gla_s = _gla(rest, la, state_gla[l], g_gla_norm[l], go,
                         row0=T, nseq=SB, nc=1, C=ST, DK=DK, DV=DV)

        pcol = (2 * KW + 2 * GW) // PW
        bp = _pick(T, 512)
        po, hist_p = _pool(rest, pcol, zero_hist, w_pool16[l], s_pool[l], None,
                           row0=0, nseq=1, nb=T // bp, bm=bp, pos0=0)
        po, hist_s = _pool(rest, pcol, sample_hist[l], w_pool16[l], s_pool[l], po,
                           row0=T, nseq=SB, nb=1, bm=ST, pos0=P)

        y = _matmul_cat([fo, go, po], w_out16, l, name="proj_out")
        if h is None:
            hc = None
            for xg, row0 in x_groups:
                hc = _norm_mid(xg, y, g_post_mix[l], g_pre_mlp[l], hc, row0)
            h, c = hc
        else:
            h, c = _norm_mid(h, y, g_post_mix[l], g_pre_mlp[l])
        (hid,) = _matmul(c, w_up16, l, [BF16], name="mlp_up", relu2=True)
        f = _matmul_kacc(hid, w_down16, l, name="mlp_down")
        if l + 1 < L:
            h, a = _norm_mid(h, f, g_post_mlp[l], g_pre_mix[l + 1])
        else:
            y_prompt = _norm_last(h, f, g_post_mlp[l], 0, T)
            y_sample = _norm_last(h, f, g_post_mlp[l], T, MS)

        outs_p[2].append(logf[:T, :H].reshape(1, T, H))
        outs_p[3].append(gla_p)
        outs_p[4].append(hist_p[:, POOL_CARRY - POOL_HIST:])
        outs_s[2].append(logf[T:, :H].reshape(SB, ST, H))
        outs_s[3].append(gla_s)
        outs_s[4].append(hist_s[:, POOL_CARRY - POOL_HIST:])

    p_kv = [x.reshape(L, 1, T, H, FOX_HD) for x in kv_p]
    s_kv = [x.reshape(L, SB, ST, H, FOX_HD) for x in kv_s]
    return (y_prompt.reshape(1, T, D), y_sample.reshape(SB, ST, D),
            *p_kv, *[jnp.stack(o) for o in outs_p[2:]], *s_kv, *[jnp.stack(o) for o in outs_s[2:]])
```

```python
import functools
import math

import jax
import jax.numpy as jnp
from jax import lax
from jax.experimental import pallas as pl
from jax.experimental.pallas import tpu as pltpu

F32 = jnp.float32
BF16 = jnp.bfloat16

RMS_EPS = 1e-6
FOX_HD = 128
GLA_HEADS = 4
GLA_TAU = 16.0
GLA_CHUNK = 64
GLA_SUB_LOG2 = 4
GLA_SUB = 1 << GLA_SUB_LOG2
POOL_WINDOWS = (2, 4, 8, 16)
POOL_HIST = 15
POOL_CARRY = 16
LANES = 128
SUBLANES = 8
ONES_ROWS = 16
LOG2E = 1.4426950408889634
BIAS_TERMS = 3
VMEM_LIMIT = 56 * 1024 * 1024


def _params(*sem):
    return pltpu.CompilerParams(dimension_semantics=sem, vmem_limit_bytes=VMEM_LIMIT)


def _pick(n, pref, mult=8):
    b = min(pref, n)
    b -= b % mult
    while b > 0 and n % b:
        b -= mult
    assert b > 0, (n, pref, mult)
    return b


def _rms(x):
    return x * lax.rsqrt(jnp.mean(x * x, axis=-1, keepdims=True) + RMS_EPS)


def _log_sigmoid(x):
    return jnp.minimum(x, 0.0) - jnp.log1p(jnp.exp(-jnp.abs(x)))


def _norm_first_body(h_ref, g_ref, a_ref):
    a_ref[...] = (_rms(h_ref[...]) * g_ref[...]).astype(a_ref.dtype)


def _norm_mid_body(h_ref, y_ref, gy_ref, g_ref, h_out, a_out):
    h = h_ref[...] + _rms(y_ref[...]) * gy_ref[...]
    h_out[...] = h
    a_out[...] = (_rms(h) * g_ref[...]).astype(a_out.dtype)


def _norm_last_body(h_ref, y_ref, gy_ref, h_out):
    h_out[...] = h_ref[...] + _rms(y_ref[...]) * gy_ref[...]


def _row_block(row0, nrows, pref):
    return _pick(math.gcd(row0, nrows) if row0 else nrows, pref)


def _norm_first(x, g, prev, row0, M):
    nrows, D = x.shape
    bm = _row_block(row0, nrows, 256)
    r0 = row0 // bm
    in_specs = [pl.BlockSpec((bm, D), lambda i: (i, 0)), pl.BlockSpec((1, D), lambda i: (0, 0))]
    args = [x, g.reshape(1, D)]
    if prev is not None:
        in_specs.append(pl.BlockSpec(memory_space=pl.ANY))
        args.append(prev)
    return pl.pallas_call(
        lambda x_ref, g_ref, *rest: _norm_first_body(x_ref, g_ref, rest[-1]),
        grid=(nrows // bm,), in_specs=in_specs, out_specs=pl.BlockSpec((bm, D), lambda i: (r0 + i, 0)),
        out_shape=jax.ShapeDtypeStruct((M, D), BF16),
        input_output_aliases={2: 0} if prev is not None else {},
        compiler_params=_params("parallel"), name="norm_first")(*args)


def _norm_mid(h_src, y, gy, g, prev=None, row0=0):
    nrows, D = h_src.shape
    M = y.shape[0]
    bm = _row_block(row0, nrows, 256)
    r0 = row0 // bm
    src = pl.BlockSpec((bm, D), lambda i: (i, 0))
    row = pl.BlockSpec((bm, D), lambda i: (r0 + i, 0))
    vec = pl.BlockSpec((1, D), lambda i: (0, 0))
    in_specs = [src, row, vec, vec]
    args = [h_src, y, gy.reshape(1, D), g.reshape(1, D)]
    if prev is not None:
        in_specs += [pl.BlockSpec(memory_space=pl.ANY)] * 2
        args += list(prev)
    return pl.pallas_call(
        lambda h_ref, y_ref, gy_ref, g_ref, *rest: _norm_mid_body(h_ref, y_ref, gy_ref, g_ref, *rest[-2:]),
        grid=(nrows // bm,), in_specs=in_specs, out_specs=[row, row],
        out_shape=[jax.ShapeDtypeStruct((M, D), F32), jax.ShapeDtypeStruct((M, D), BF16)],
        input_output_aliases={4: 0, 5: 1} if prev is not None else {},
        compiler_params=_params("parallel"), name="norm_mid")(*args)


def _norm_last(h, y, gy, row0, nrows):
    D = h.shape[1]
    bm = _pick(math.gcd(row0, nrows) if row0 else nrows, 256)
    r0 = row0 // bm
    row = pl.BlockSpec((bm, D), lambda i: (r0 + i, 0))
    vec = pl.BlockSpec((1, D), lambda i: (0, 0))
    return pl.pallas_call(
        _norm_last_body, grid=(nrows // bm,), in_specs=[row, row, vec],
        out_specs=pl.BlockSpec((bm, D), lambda i: (i, 0)),
        out_shape=jax.ShapeDtypeStruct((nrows, D), F32),
        compiler_params=_params("parallel"), name="norm_last")(h, y, gy.reshape(1, D))


def _mm_body(x_ref, w_ref, *o_refs, relu2, scale):
    acc = jnp.dot(x_ref[...], w_ref[...], preferred_element_type=F32)
    if relu2:
        acc = jnp.square(jnp.maximum(acc, 0.0))
    if scale is not None:
        acc = acc * scale
    for o in o_refs:
        o[...] = acc.astype(o.dtype)


def _matmul(x, w, layer, out_dtypes, *, name, relu2=False, scale=None, col0=0, ncols=None, bm=1024, bn=1024):
    M, K = x.shape
    N = w.shape[2] - col0 if ncols is None else ncols
    bm = _pick(M, bm)
    bn = _pick(N, bn, LANES)
    assert col0 % bn == 0
    c0 = col0 // bn
    outs = pl.pallas_call(
        functools.partial(_mm_body, relu2=relu2, scale=scale), grid=(M // bm, N // bn),
        in_specs=[pl.BlockSpec((bm, K), lambda i, j: (i, 0)),
                  pl.BlockSpec((None, K, bn), lambda i, j: (layer, 0, c0 + j))],
        out_specs=[pl.BlockSpec((bm, bn), lambda i, j: (i, j)) for _ in out_dtypes],
        out_shape=[jax.ShapeDtypeStruct((M, N), dt) for dt in out_dtypes],
        compiler_params=_params("parallel", "arbitrary"), name=name)(x, w)
    return outs


def _mm_heads_body(x_ref, w_ref, *refs, hd):
    o32_ref, o16_ref = refs[-2:]
    acc = jnp.dot(x_ref[...], w_ref[...], preferred_element_type=F32)
    o16_ref[...] = acc.astype(o16_ref.dtype)
    o32_ref[...] = pltpu.einshape("m(hd)->mhd", acc, d=hd)


def _matmul_heads(x, w, prev32, prev16, *, name, col0, ncols, row0, nrows, layer, nlayers, hd, bm=512, bn=1024):
    M, K = x.shape
    bm = _pick(math.gcd(row0, nrows) if row0 else nrows, bm)
    bn = _pick(ncols, bn, LANES * SUBLANES)
    assert col0 % bn == 0 and bn % hd == 0
    c0, r0, hb = col0 // bn, row0 // bm, bn // hd
    in_specs = [pl.BlockSpec((bm, K), lambda i, j: (r0 + i, 0)),
                pl.BlockSpec((None, K, bn), lambda i, j: (layer, 0, c0 + j))]
    args = [x, w]
    aliases = {}
    for out_idx, prev in enumerate((prev32, prev16)):
        if prev is not None:
            aliases[len(args)] = out_idx
            in_specs.append(pl.BlockSpec(memory_space=pl.ANY))
            args.append(prev)
    return pl.pallas_call(
        functools.partial(_mm_heads_body, hd=hd), grid=(nrows // bm, ncols // bn), in_specs=in_specs,
        out_specs=[pl.BlockSpec((None, bm, hb, hd), lambda i, j: (layer, i, j, 0)),
                   pl.BlockSpec((bm, bn), lambda i, j: (r0 + i, j))],
        out_shape=[jax.ShapeDtypeStruct((nlayers, nrows, ncols // hd, hd), F32),
                   jax.ShapeDtypeStruct((M, ncols), BF16)],
        input_output_aliases=aliases,
        compiler_params=_params("parallel", "arbitrary"), name=name)(*args)


def _mm_cat_body(*refs, splits):
    x_refs, w_ref, o_ref = refs[:-2], refs[-2], refs[-1]
    acc = None
    off = 0
    for x_ref, kk in zip(x_refs, splits):
        part = jnp.dot(x_ref[...], w_ref[off:off + kk, :], preferred_element_type=F32)
        acc = part if acc is None else acc + part
        off += kk
    o_ref[...] = acc


def _matmul_cat(xs, w, layer, *, name, bm=1024, bn=1024):
    M = xs[0].shape[0]
    _, K, N = w.shape
    splits = tuple(x.shape[1] for x in xs)
    assert sum(splits) == K
    bm = _pick(M, bm)
    bn = _pick(N, bn, LANES)
    in_specs = [pl.BlockSpec((bm, kk), lambda i, j: (i, 0)) for kk in splits]
    in_specs.append(pl.BlockSpec((None, K, bn), lambda i, j: (layer, 0, j)))
    return pl.pallas_call(
        functools.partial(_mm_cat_body, splits=splits), grid=(M // bm, N // bn),
        in_specs=in_specs, out_specs=pl.BlockSpec((bm, bn), lambda i, j: (i, j)),
        out_shape=jax.ShapeDtypeStruct((M, N), F32),
        compiler_params=_params("parallel", "arbitrary"), name=name)(*xs, w)


def _mm_kacc_body(x_ref, w_ref, o_ref, *, bn):
    k = pl.program_id(1)

    @pl.when(k == 0)
    def _():
        o_ref[...] = jnp.zeros(o_ref.shape, F32)

    x = x_ref[...]
    for n0 in range(0, o_ref.shape[1], bn):
        o_ref[:, n0:n0 + bn] += jnp.dot(x, w_ref[:, n0:n0 + bn], preferred_element_type=F32)


def _matmul_kacc(x, w, layer, *, name, bm=512, bk=2048):
    M, K = x.shape
    N = w.shape[2]
    bm = _pick(M, bm)
    bk = _pick(K, bk, LANES)
    return pl.pallas_call(
        functools.partial(_mm_kacc_body, bn=_pick(N, 512, LANES)), grid=(M // bm, K // bk),
        in_specs=[pl.BlockSpec((bm, bk), lambda i, k: (i, k)),
                  pl.BlockSpec((None, bk, N), lambda i, k: (layer, k, 0))],
        out_specs=pl.BlockSpec((bm, N), lambda i, k: (i, 0)),
        out_shape=jax.ShapeDtypeStruct((M, N), F32),
        compiler_params=_params("parallel", "arbitrary"), name=name)(x, w)


def _gates_body(t_ref, bf_ref, w2_ref, bg_ref, logf_ref, la_ref):
    t = t_ref[...]
    logf_ref[...] = _log_sigmoid(t + bf_ref[...])
    zg = jnp.dot(t, w2_ref[...], preferred_element_type=F32) + bg_ref[...]
    la_ref[...] = _log_sigmoid(zg) * (1.0 / GLA_TAU)


def _gates(tail, bfox_pad, w2_pad, bgla):
    M = tail.shape[0]
    KW = w2_pad.shape[1]
    bm = _pick(M, 512)
    return pl.pallas_call(
        _gates_body, grid=(M // bm,),
        in_specs=[pl.BlockSpec((bm, LANES), lambda i: (i, 0)),
                  pl.BlockSpec((1, LANES), lambda i: (0, 0)),
                  pl.BlockSpec((LANES, KW), lambda i: (0, 0)),
                  pl.BlockSpec((1, KW), lambda i: (0, 0))],
        out_specs=[pl.BlockSpec((bm, LANES), lambda i: (i, 0)),
                   pl.BlockSpec((bm, KW), lambda i: (i, 0))],
        out_shape=[jax.ShapeDtypeStruct((M, LANES), F32), jax.ShapeDtypeStruct((M, KW), F32)],
        compiler_params=_params("parallel"), name="gates")(tail, bfox_pad, w2_pad, bgla)


def _row_cumsum(x):
    n = x.shape[0]
    row = lax.broadcasted_iota(jnp.int32, x.shape, 0)
    sh = 1
    while sh < n:
        x = x + jnp.where(row >= sh, pltpu.roll(x, sh, axis=0), 0.0)
        sh *= 2
    return x


def _cumsum_body(x_ref, c_ref, carry_sc):
    @pl.when(pl.program_id(1) == 0)
    def _():
        carry_sc[...] = jnp.zeros(carry_sc.shape, F32)

    c = _row_cumsum(x_ref[...]) + carry_sc[...]
    c_ref[...] = c
    n = c.shape[0]
    carry_sc[...] = c[n - 1:n, :]


def _cumsum(x, bl):
    S, L, W = x.shape
    bl = _pick(L, bl)
    spec = pl.BlockSpec((None, bl, W), lambda s, i: (s, i, 0))
    return pl.pallas_call(
        _cumsum_body, grid=(S, L // bl), in_specs=[spec], out_specs=spec,
        out_shape=jax.ShapeDtypeStruct((S, L, W), F32),
        scratch_shapes=[pltpu.VMEM((1, W), F32)],
        compiler_params=_params("parallel", "arbitrary"), name="cumsum")(x)


_NT = (((1,), (1,)), ((), ()))
_TN = (((0,), (0,)), ((), ()))


def _fox_prompt_body(q_ref, k_ref, kb_ref, vt_ref, o_ref, acc_sc, st_sc, qa_sc, *, tq, tk, nsub):
    qi = pl.program_id(1)
    hd = vt_ref.shape[0]
    bq = nsub * tq
    nkb = bq // tk
    ones = jnp.ones((ONES_ROWS, tk), BF16)
    lane = lax.broadcasted_iota(jnp.int32, (bq, LANES), 1) - pl.program_id(0) * BIAS_TERMS
    qa_sc[:, :hd] = q_ref[...]
    qa_sc[:, hd:] = jnp.where(lane.astype(jnp.uint32) < BIAS_TERMS, 1.0, 0.0).astype(BF16)

    def mask(st, mask_off):
        key = lax.broadcasted_iota(jnp.int32, (tk, tq), 0) + mask_off
        qry = lax.broadcasted_iota(jnp.int32, (tk, tq), 1)
        return jnp.where(key <= qry, st, -jnp.inf)

    def scores(j, sub):
        s0 = pl.multiple_of(j * tk, tk)
        k = jnp.concatenate([k_ref[pl.ds(s0, tk), :], kb_ref[pl.ds(s0, tk), :]], axis=1)
        return lax.dot_general(k, qa_sc[sub * tq:(sub + 1) * tq, :], _NT,
                               preferred_element_type=F32)

    def accumulate(j, sub, st, m_blk, m_prev):
        s0 = pl.multiple_of(j * tk, tk)
        m_new = jnp.maximum(m_prev, m_blk)
        alpha = jnp.exp2(m_prev - m_new)
        pt = jnp.exp2(st - m_new).astype(BF16)
        vt = jnp.concatenate([vt_ref[:, pl.ds(s0, tk)], ones], axis=0)
        acc_sc[sub] = alpha * acc_sc[sub] + jnp.dot(vt, pt, preferred_element_type=F32)
        return m_new

    def issue(j, slot):
        mb = []
        for sub in range(nsub):
            st = scores(j, sub)
            st_sc[slot, sub] = st
            mb.append(jnp.max(st, axis=0, keepdims=True))
        return tuple(mb)

    acc_sc[...] = jnp.zeros(acc_sc.shape, F32)
    m0 = tuple(jnp.full((1, tq), -jnp.inf, F32) for _ in range(nsub))
    n = nkb * qi

    def body(jj, carry):
        ms, mb = carry
        for u in range(nkb):
            slot = u % 2
            j = nkb * jj + u
            mb_next = issue(j + 1, 1 - slot)
            ms = tuple(accumulate(j, sub, st_sc[slot, sub], mb[sub], ms[sub]) for sub in range(nsub))
            mb = mb_next
        return ms, mb

    assert nkb % 2 == 0
    ms, _ = lax.fori_loop(0, qi, body, (m0, issue(0, 0)))
    ms = list(ms)
    band = [(d, sub) for d in range(nkb) for sub in range(nsub) if d * tk <= (sub + 1) * tq - 1]
    late = {(d, sub): scores(n + d, sub) for d, sub in band if d > 0}
    for d, sub in band:
        k_lo, k_hi, q_lo = d * tk, (d + 1) * tk - 1, sub * tq
        st = st_sc[0, sub] if d == 0 else late[d, sub]
        if k_hi > q_lo:
            st = mask(st, k_lo - q_lo)
        ms[sub] = accumulate(n + d, sub, st, jnp.max(st, axis=0, keepdims=True), ms[sub])
    for sub in range(nsub):
        acc = acc_sc[sub]
        o = acc[:hd, :] / acc[hd:hd + 1, :]
        o_ref[sub * tq:(sub + 1) * tq, :] = jnp.transpose(o).astype(o_ref.dtype)


FOX_TQ, FOX_TK, FOX_NSUB = 512, 512, 4


def _fox_prompt(q16, k16, kb16, vt16, T):
    M, W = q16.shape
    H = W // FOX_HD
    tq, tk, nsub = FOX_TQ, FOX_TK, FOX_NSUB
    bq = nsub * tq
    assert T % bq == 0 and bq % tk == 0 and FOX_HD == LANES and H * BIAS_TERMS <= LANES
    return pl.pallas_call(
        functools.partial(_fox_prompt_body, tq=tq, tk=tk, nsub=nsub), grid=(H, T // bq),
        in_specs=[pl.BlockSpec((bq, FOX_HD), lambda h, i: (i, h)),
                  pl.BlockSpec((T, FOX_HD), lambda h, i: (0, h)),
                  pl.BlockSpec((T, LANES), lambda h, i: (0, 0)),
                  pl.BlockSpec((FOX_HD, T), lambda h, i: (h, 0))],
        out_specs=pl.BlockSpec((bq, FOX_HD), lambda h, i: (i, h)),
        out_shape=jax.ShapeDtypeStruct((M, W), BF16),
        scratch_shapes=[pltpu.VMEM((nsub, FOX_HD + ONES_ROWS, tq), F32),
                        pltpu.VMEM((2, nsub, tk, tq), F32),
                        pltpu.VMEM((bq, FOX_HD + LANES), BF16)],
        compiler_params=_params("parallel", "arbitrary"), name="fox_prompt")(q16, k16, kb16, vt16)


def _split_bf16(x, terms):
    parts = []
    for _ in range(terms):
        top = lax.bitcast_convert_type(lax.bitcast_convert_type(x, jnp.uint32) & jnp.uint32(0xFFFF0000), F32)
        parts.append(top.astype(BF16))
        x = x - top
    return jnp.stack(parts, axis=-1)


def _fox_sample_body(q_ref, kn_ref, vn_ref, kp_ref, vp_ref, c_ref, fo_any, o_ref, *, P, ST, HB):
    del fo_any
    r = lax.broadcasted_iota(jnp.int32, (ST, ST), 0)
    cc = lax.broadcasted_iota(jnp.int32, (ST, ST), 1)
    hd = FOX_HD
    kp = pltpu.einshape("phd->hpd", kp_ref[...]).astype(BF16)
    vp = pltpu.einshape("phd->hpd", vp_ref[...]).astype(BF16)
    scores = []
    for h in range(HB):
        sl = slice(h * hd, (h + 1) * hd)
        q = q_ref[:, sl]
        c = c_ref[h]
        cb = c[:, P - 1:P]
        sp = (lax.dot_general(q, kp[h], _NT, preferred_element_type=F32)
              + (cb - c[:, :P]) * LOG2E)
        sn = (lax.dot_general(q, kn_ref[:, sl], _NT, preferred_element_type=F32)
              + (cb - c[:, P:]) * LOG2E)
        scores.append((sp, jnp.where(cc <= r, sn, -jnp.inf)))
    for h, (sp, sn) in enumerate(scores):
        sl = slice(h * hd, (h + 1) * hd)
        m = jnp.maximum(jnp.max(sp, axis=1, keepdims=True), jnp.max(sn, axis=1, keepdims=True))
        pp = jnp.exp2(sp - m)
        pn = jnp.exp2(sn - m)
        l = jnp.sum(pp, axis=1, keepdims=True) + jnp.sum(pn, axis=1, keepdims=True)
        o = (jnp.dot(pp.astype(BF16), vp[h], preferred_element_type=F32)
             + jnp.dot(pn.astype(BF16), vn_ref[:, sl], preferred_element_type=F32))
        o_ref[:, sl] = (o / l).astype(o_ref.dtype)


def _fox_sample(fo, q16, k16, v16, cache_k, cache_v, ct, layer, T, SB, ST):
    M, W = q16.shape
    H = W // FOX_HD
    P = cache_k.shape[2]
    r0 = T // ST
    HB = SUBLANES
    assert H % HB == 0
    new = pl.BlockSpec((ST, HB * FOX_HD), lambda b, h: (r0 + b, h))
    past = pl.BlockSpec((None, None, P, HB, FOX_HD), lambda b, h: (layer, b, 0, h, 0))
    return pl.pallas_call(
        functools.partial(_fox_sample_body, P=P, ST=ST, HB=HB), grid=(SB, H // HB),
        in_specs=[new, new, new, past, past,
                  pl.BlockSpec((None, HB, 1, P + ST), lambda b, h: (b, h, 0, 0)),
                  pl.BlockSpec(memory_space=pl.ANY)],
        out_specs=new,
        out_shape=jax.ShapeDtypeStruct((M, W), BF16),
        input_output_aliases={6: 0},
        compiler_params=_params("parallel", "arbitrary"), name="fox_sample")(
            q16, k16, v16, cache_k, cache_v, ct, fo)


def _gla_body(q_ref, k_ref, v_ref, r_ref, la_ref, s0_ref, g_ref, go_any, o_ref, sout_ref,
              S_sc, b_sc, q_sc, *, C, NH, DK, DV, nc):
    del go_any
    c = pl.program_id(1)

    @pl.when(c == 0)
    def _():
        S_sc[...] = s0_ref[...]

    b = _row_cumsum(la_ref[...])
    b_sc[...] = b
    q = q_ref[...] * (DK ** -0.5)
    q_sc[...] = q
    bl = b[C - 1:C, :]
    qe = (q * jnp.exp(b)).astype(BF16)
    kdec = (k_ref[...] * jnp.exp(bl - b)).astype(BF16)
    ebl = jnp.exp(bl)
    v16 = v_ref[...].astype(BF16)

    R = GLA_SUB
    nb = C // R
    k = k_ref[...]
    srow_c = lax.broadcasted_iota(jnp.int32, b.shape, 0)
    at_off = [None] * NH
    for i in range(1, nb):
        b_ref_i = b[R * i - 1:R * i, :]
        in_blk = (srow_c >> GLA_SUB_LOG2) == i
        qt = (q * jnp.exp(jnp.where(in_blk, b - b_ref_i, -jnp.inf))).astype(BF16)
        kt = (k * jnp.exp(jnp.where(srow_c < R * i, b_ref_i - b, -jnp.inf))).astype(BF16)
        for h in range(NH):
            sl = slice(h * DK, (h + 1) * DK)
            part = lax.dot_general(kt[:, sl], qt[:, sl], _NT, preferred_element_type=F32)
            at_off[h] = part if at_off[h] is None else at_off[h] + part

    srow = lax.broadcasted_iota(jnp.int32, (R, DK), 0)
    lane = lax.broadcasted_iota(jnp.int32, (R, C), 1)
    at_diag = [[] for _ in range(NH)]
    for i in range(nb):
        blk = slice(R * i, R * (i + 1))
        ats = [jnp.zeros((R, C), F32) for _ in range(NH)]
        for r in range(R):
            t = R * i + r
            mask = srow <= r
            lm = lane == t
            for h in range(NH):
                sl = slice(h * DK, (h + 1) * DK)
                e = jnp.exp(jnp.where(mask, b_sc[t:t + 1, sl] - b_sc[blk, sl], -jnp.inf))
                w = e * (q_sc[t:t + 1, sl] * k_ref[blk, sl])
                ats[h] = jnp.where(lm, jnp.sum(w, axis=1, keepdims=True), ats[h])
        for h in range(NH):
            at_diag[h].append(ats[h])
    ats = []
    for h in range(NH):
        at = jnp.concatenate(at_diag[h], axis=0)
        ats.append(at if at_off[h] is None else at + at_off[h])

    g = g_ref[...]
    for h in range(NH):
        sl = slice(h * DK, (h + 1) * DK)
        vs = slice(h * DV, (h + 1) * DV)
        S = S_sc[h]
        o = (jnp.dot(qe[:, sl], S.astype(BF16), preferred_element_type=F32)
             + lax.dot_general(ats[h].astype(BF16), v16[:, vs], _TN, preferred_element_type=F32))
        r = r_ref[:, vs]
        o = _rms(o) * g * (r * (1.0 / (1.0 + jnp.exp(-r))))
        o_ref[:, vs] = o.astype(o_ref.dtype)
        dm = jnp.transpose(jnp.broadcast_to(ebl[:, sl], (DK, DK)))
        decay = jnp.concatenate([dm] * (DV // DK), axis=1)
        S_sc[h] = decay * S + lax.dot_general(kdec[:, sl], v16[:, vs], _TN, preferred_element_type=F32)

    @pl.when(c == nc - 1)
    def _():
        sout_ref[...] = S_sc[...]


def _gla(rest, la, s0, g, go, *, row0, nseq, nc, C, DK, DV):
    M = rest.shape[0]
    NH = GLA_HEADS
    KW, VW = NH * DK, NH * DV
    r0 = row0 // C
    assert row0 % C == 0 and VW == 2 * KW

    def rows(col):
        return lambda s, c: (r0 + s * nc + c, col)

    in_specs = [pl.BlockSpec((C, KW), rows(0)), pl.BlockSpec((C, KW), rows(1)),
                pl.BlockSpec((C, VW), rows(1)), pl.BlockSpec((C, VW), rows(2)),
                pl.BlockSpec((C, KW), rows(0)),
                pl.BlockSpec((None, NH, DK, DV), lambda s, c: (s, 0, 0, 0)),
                pl.BlockSpec((1, DV), lambda s, c: (0, 0))]
    args = [rest, rest, rest, rest, la, s0, g.reshape(1, DV)]
    aliases = {}
    if go is not None:
        in_specs.append(pl.BlockSpec(memory_space=pl.ANY))
        args.append(go)
        aliases = {7: 0}
        body = _gla_body
    else:
        body = functools.partial(_gla_body_noalias)
    return pl.pallas_call(
        functools.partial(body, C=C, NH=NH, DK=DK, DV=DV, nc=nc), grid=(nseq, nc),
        in_specs=in_specs,
        out_specs=[pl.BlockSpec((C, VW), rows(0)),
                   pl.BlockSpec((None, NH, DK, DV), lambda s, c: (s, 0, 0, 0))],
        out_shape=[jax.ShapeDtypeStruct((M, VW), BF16),
                   jax.ShapeDtypeStruct((nseq, NH, DK, DV), F32)],
        scratch_shapes=[pltpu.VMEM((NH, DK, DV), F32), pltpu.VMEM((C, KW), F32),
                        pltpu.VMEM((C, KW), F32)],
        input_output_aliases=aliases,
        compiler_params=_params("parallel", "arbitrary"), name="gla")(*args)


def _gla_body_noalias(q_ref, k_ref, v_ref, r_ref, la_ref, s0_ref, g_ref, o_ref, sout_ref,
                      S_sc, b_sc, q_sc, **kw):
    _gla_body(q_ref, k_ref, v_ref, r_ref, la_ref, s0_ref, g_ref, None, o_ref, sout_ref,
              S_sc, b_sc, q_sc, **kw)


def _pool_body(*refs, bm, pos0, GC, aliased):
    if aliased:
        p_ref, hist_ref, w_ref, sp_ref, _, o_ref, hout_ref, xa_sc = refs
    else:
        p_ref, hist_ref, w_ref, sp_ref, o_ref, hout_ref, xa_sc = refs
    i = pl.program_id(1)

    @pl.when(i == 0)
    def _():
        xa_sc[0:POOL_CARRY, :] = hist_ref[...]

    xa_sc[POOL_CARRY:POOL_CARRY + bm, :] = p_ref[...]
    pos = pos0 + i * bm + lax.broadcasted_iota(jnp.int32, (bm, 1), 0)
    for gi, w in enumerate(POOL_WINDOWS):
        cs = slice(gi * GC, (gi + 1) * GC)
        x = xa_sc[POOL_CARRY:POOL_CARRY + bm, cs]
        tot = x
        for d in range(1, w):
            tot = tot + xa_sc[POOL_CARRY - d:POOL_CARRY - d + bm, cs]
        cnt = jnp.minimum(pos + 1, w).astype(F32)
        dd = (tot / cnt - x).astype(BF16)
        y = jnp.dot(dd, w_ref[gi], preferred_element_type=F32) * sp_ref[:, cs]
        o_ref[:, cs] = y.astype(o_ref.dtype)
    last = xa_sc[bm:bm + POOL_CARRY, :]
    hout_ref[...] = last
    xa_sc[0:POOL_CARRY, :] = last


def _pool(rest, col, hist, w16, sp, po, *, row0, nseq, nb, bm, pos0):
    M = rest.shape[0]
    G, GC, _ = w16.shape
    PW = G * GC
    r0 = row0 // bm
    assert row0 % bm == 0 and bm >= POOL_CARRY
    in_specs = [pl.BlockSpec((bm, PW), lambda s, i: (r0 + s * nb + i, col)),
                pl.BlockSpec((None, POOL_CARRY, PW), lambda s, i: (s, 0, 0)),
                pl.BlockSpec((G, GC, GC), lambda s, i: (0, 0, 0)),
                pl.BlockSpec((1, PW), lambda s, i: (0, 0))]
    args = [rest, hist, w16, sp.reshape(1, PW)]
    aliases = {}
    if po is not None:
        in_specs.append(pl.BlockSpec(memory_space=pl.ANY))
        args.append(po)
        aliases = {4: 0}
    return pl.pallas_call(
        functools.partial(_pool_body, bm=bm, pos0=pos0, GC=GC, aliased=po is not None),
        grid=(nseq, nb), in_specs=in_specs,
        out_specs=[pl.BlockSpec((bm, PW), lambda s, i: (r0 + s * nb + i, 0)),
                   pl.BlockSpec((None, POOL_CARRY, PW), lambda s, i: (s, 0, 0))],
        out_shape=[jax.ShapeDtypeStruct((M, PW), BF16),
                   jax.ShapeDtypeStruct((nseq, POOL_CARRY, PW), F32)],
        scratch_shapes=[pltpu.VMEM((POOL_CARRY + bm, PW), F32)],
        input_output_aliases=aliases,
        compiler_params=_params("parallel", "arbitrary"), name="pool")(*args)


def kernel(x_prompt, x_sample, cache_fox_k, cache_fox_v, cache_fox_logf, state_gla, state_pool, g_pre_mix, w_in, b_fox_f, w_gla_a2, b_gla_a, g_gla_norm, w_pool, s_pool, w_out, g_post_mix, g_pre_mlp, w_up, w_down, g_post_mlp):
    B, T, D = x_prompt.shape
    SB, ST, _ = x_sample.shape
    L = w_in.shape[0]
    P = cache_fox_k.shape[2]
    assert B == 1
    FW = D // 2
    H = FW // FOX_HD
    GW = D // 4
    DV = GW // GLA_HEADS
    DK = DV // 2
    KW = GLA_HEADS * DK
    RANK = w_gla_a2.shape[1]
    PW = D - FW - GW
    GC = PW // len(POOL_WINDOWS)
    MS = SB * ST
    M = T + MS
    assert H + RANK <= LANES and 2 * KW + 2 * GW == 3 * PW and PW % LANES == 0

    o_fq, o_fk, o_fv, o_ff = 0, FW, 2 * FW, 3 * FW
    o_gq = o_ff + H
    o_ga = o_gq + 2 * KW + GW
    o_gr = o_ga + RANK

    w_in16 = w_in.astype(BF16)
    w_rest16 = jnp.concatenate([w_in16[:, :, o_gq:o_ga], w_in16[:, :, o_gr:]], axis=2)
    w_tail16 = jnp.concatenate([w_in16[:, :, o_ff:o_gq], w_in16[:, :, o_ga:o_gr],
                                jnp.zeros((L, D, LANES - H - RANK), BF16)], axis=2)
    w_out16, w_up16, w_down16, w_pool16 = (w.astype(BF16) for w in (w_out, w_up, w_down, w_pool))
    past_logf = jnp.pad(cache_fox_logf.astype(F32), ((0, 0), (0, 0), (0, 0), (0, LANES - H)))
    zero_state = jnp.zeros((1, GLA_HEADS, DK, DV), F32)
    zero_hist = jnp.zeros((1, POOL_CARRY, PW), F32)
    sample_hist = jnp.pad(state_pool, ((0, 0), (0, 0), (POOL_CARRY - POOL_HIST, 0), (0, 0)))

    h = None
    x_groups = ((x_prompt.reshape(T, D), 0), (x_sample.reshape(MS, D), T))
    a = None
    for xg, row0 in x_groups:
        a = _norm_first(xg, g_pre_mix[0], a, row0, M)
    outs_p = [[] for _ in range(5)]
    outs_s = [[] for _ in range(5)]
    kv_p, kv_s = [None, None], [None, None]
    for l in range(L):
        bfox_pad = jnp.pad(b_fox_f[l].astype(F32), (0, LANES - H)).reshape(1, LANES)
        w2_pad = jnp.pad(w_gla_a2[l].astype(F32), ((H, LANES - H - RANK), (0, 0)))

        (q16,) = _matmul(a, w_in16, l, [BF16], name="proj_q", col0=o_fq, ncols=FW,
                         scale=FOX_HD ** -0.5 * LOG2E)
        kv16 = []
        for t, (name, col0) in enumerate((("proj_k", o_fk), ("proj_v", o_fv))):
            common = dict(name=name, col0=col0, ncols=FW, layer=l, nlayers=L, hd=FOX_HD)
            kv_p[t], x16 = _matmul_heads(a, w_in16, kv_p[t], None, row0=0, nrows=T, **common)
            kv_s[t], x16 = _matmul_heads(a, w_in16, kv_s[t], x16, row0=T, nrows=MS, **common)
            kv16.append(x16)
        k16, v16 = kv16
        (rest,) = _matmul(a, w_rest16, l, [F32], name="proj_rest")
        (tail,) = _matmul(a, w_tail16, l, [F32], name="proj_tail")
        logf, la = _gates(tail, bfox_pad, w2_pad, b_gla_a[l].reshape(1, KW))

        c_p = _cumsum(logf[:T].reshape(1, T, LANES), 1024)
        bias = _split_bf16(c_p[0, :, :H] * (-LOG2E), BIAS_TERMS)
        kb16 = jnp.pad(bias.reshape(T, H * BIAS_TERMS), ((0, 0), (0, LANES - H * BIAS_TERMS)))
        vt16 = jnp.transpose(v16[:T])
        fo = _fox_prompt(q16, k16, kb16, vt16, T)
        lf_all = jnp.concatenate([past_logf[l], logf[T:].reshape(SB, ST, LANES)], axis=1)
        c_s = _cumsum(lf_all, P + ST)
        ct_s = jnp.transpose(c_s[:, :, :H], (0, 2, 1)).reshape(SB, H, 1, P + ST)
        fo = _fox_sample(fo, q16, k16, v16, cache_fox_k, cache_fox_v, ct_s, l, T, SB, ST)

        go, gla_p = _gla(rest, la, zero_state, g_gla_norm[l], None,
                         row0=0, nseq=1, nc=T // GLA_CHUNK, C=GLA_CHUNK, DK=DK, DV=DV)
        go, gla_s = _gla(rest, la, state_gla[l], g_gla_norm[l], go,
                         row0=T, nseq=SB, nc=1, C=ST, DK=DK, DV=DV)

        pcol = (2 * KW + 2 * GW) // PW
        bp = _pick(T, 512)
        po, hist_p = _pool(rest, pcol, zero_hist, w_pool16[l], s_pool[l], None,
                           row0=0, nseq=1, nb=T // bp, bm=bp, pos0=0)
        po, hist_s = _pool(rest, pcol, sample_hist[l], w_pool16[l], s_pool[l], po,
                           row0=T, nseq=SB, nb=1, bm=ST, pos0=P)

        y = _matmul_cat([fo, go, po], w_out16, l, name="proj_out")
        if h is None:
            hc = None
            for xg, row0 in x_groups:
                hc = _norm_mid(xg, y, g_post_mix[l], g_pre_mlp[l], hc, row0)
            h, c = hc
        else:
            h, c = _norm_mid(h, y, g_post_mix[l], g_pre_mlp[l])
        (hid,) = _matmul(c, w_up16, l, [BF16], name="mlp_up", relu2=True)
        f = _matmul_kacc(hid, w_down16, l, name="mlp_down")
        if l + 1 < L:
            h, a = _norm_mid(h, f, g_post_mlp[l], g_pre_mix[l + 1])
        else:
            y_prompt = _norm_last(h, f, g_post_mlp[l], 0, T)
            y_sample = _norm_last(h, f, g_post_mlp[l], T, MS)

        outs_p[2].append(logf[:T, :H].reshape(1, T, H))
        outs_p[3].append(gla_p)
        outs_p[4].append(hist_p[:, POOL_CARRY - POOL_HIST:])
        outs_s[2].append(logf[T:, :H].reshape(SB, ST, H))
        outs_s[3].append(gla_s)
        outs_s[4].append(hist_s[:, POOL_CARRY - POOL_HIST:])

    p_kv = [x.reshape(L, 1, T, H, FOX_HD) for x in kv_p]
    s_kv = [x.reshape(L, SB, ST, H, FOX_HD) for x in kv_s]
    return (y_prompt.reshape(1, T, D), y_sample.reshape(SB, ST, D),
            *p_kv, *[jnp.stack(o) for o in outs_p[2:]], *s_kv, *[jnp.stack(o) for o in outs_s[2:]])
```

```python
import functools
import math

import jax
import jax.numpy as jnp
from jax import lax
from jax.experimental import pallas as pl
from jax.experimental.pallas import tpu as pltpu

F32 = jnp.float32
BF16 = jnp.bfloat16

RMS_EPS = 1e-6
FOX_HD = 128
GLA_HEADS = 4
GLA_TAU = 16.0
GLA_CHUNK = 64
GLA_SUB_LOG2 = 4
GLA_SUB = 1 << GLA_SUB_LOG2
POOL_WINDOWS = (2, 4, 8, 16)
POOL_HIST = 15
POOL_CARRY = 16
LANES = 128
SUBLANES = 8
BF16_ROWS = 16
ONES_ROWS = BF16_ROWS
LOG2E = 1.4426950408889634
BIAS_TERMS = 3
VMEM_LIMIT = 56 * 1024 * 1024


def _params(*sem):
    return pltpu.CompilerParams(dimension_semantics=sem, vmem_limit_bytes=VMEM_LIMIT)


def _pick(n, pref, mult=8):
    b = min(pref, n)
    b -= b % mult
    while b > 0 and n % b:
        b -= mult
    assert b > 0, (n, pref, mult)
    return b


def _rms(x):
    return x * lax.rsqrt(jnp.mean(x * x, axis=-1, keepdims=True) + RMS_EPS)


def _log_sigmoid(x):
    return jnp.minimum(x, 0.0) - jnp.log1p(jnp.exp(-jnp.abs(x)))


def _norm_first_body(h_ref, g_ref, a_ref):
    a_ref[...] = (_rms(h_ref[...]) * g_ref[...]).astype(a_ref.dtype)


def _norm_mid_body(h_ref, y_ref, gy_ref, g_ref, h_out, a_out):
    h = h_ref[...] + _rms(y_ref[...]) * gy_ref[...]
    h_out[...] = h
    a_out[...] = (_rms(h) * g_ref[...]).astype(a_out.dtype)


def _norm_last_body(h_ref, y_ref, gy_ref, h_out):
    h_out[...] = h_ref[...] + _rms(y_ref[...]) * gy_ref[...]


def _row_block(row0, nrows, pref):
    return _pick(math.gcd(row0, nrows) if row0 else nrows, pref)


def _norm_first(x, g, prev, row0, M):
    nrows, D = x.shape
    bm = _row_block(row0, nrows, 256)
    r0 = row0 // bm
    in_specs = [pl.BlockSpec((bm, D), lambda i: (i, 0)), pl.BlockSpec((1, D), lambda i: (0, 0))]
    args = [x, g.reshape(1, D)]
    if prev is not None:
        in_specs.append(pl.BlockSpec(memory_space=pl.ANY))
        args.append(prev)
    return pl.pallas_call(
        lambda x_ref, g_ref, *rest: _norm_first_body(x_ref, g_ref, rest[-1]),
        grid=(nrows // bm,), in_specs=in_specs, out_specs=pl.BlockSpec((bm, D), lambda i: (r0 + i, 0)),
        out_shape=jax.ShapeDtypeStruct((M, D), BF16),
        input_output_aliases={2: 0} if prev is not None else {},
        compiler_params=_params("parallel"), name="norm_first")(*args)


def _norm_mid(h_src, y, gy, g, prev=None, row0=0):
    nrows, D = h_src.shape
    M = y.shape[0]
    bm = _row_block(row0, nrows, 256)
    r0 = row0 // bm
    src = pl.BlockSpec((bm, D), lambda i: (i, 0))
    row = pl.BlockSpec((bm, D), lambda i: (r0 + i, 0))
    vec = pl.BlockSpec((1, D), lambda i: (0, 0))
    in_specs = [src, row, vec, vec]
    args = [h_src, y, gy.reshape(1, D), g.reshape(1, D)]
    if prev is not None:
        in_specs += [pl.BlockSpec(memory_space=pl.ANY)] * 2
        args += list(prev)
    return pl.pallas_call(
        lambda h_ref, y_ref, gy_ref, g_ref, *rest: _norm_mid_body(h_ref, y_ref, gy_ref, g_ref, *rest[-2:]),
        grid=(nrows // bm,), in_specs=in_specs, out_specs=[row, row],
        out_shape=[jax.ShapeDtypeStruct((M, D), F32), jax.ShapeDtypeStruct((M, D), BF16)],
        input_output_aliases={4: 0, 5: 1} if prev is not None else {},
        compiler_params=_params("parallel"), name="norm_mid")(*args)


def _norm_last(h, y, gy, row0, nrows):
    D = h.shape[1]
    bm = _pick(math.gcd(row0, nrows) if row0 else nrows, 256)
    r0 = row0 // bm
    row = pl.BlockSpec((bm, D), lambda i: (r0 + i, 0))
    vec = pl.BlockSpec((1, D), lambda i: (0, 0))
    return pl.pallas_call(
        _norm_last_body, grid=(nrows // bm,), in_specs=[row, row, vec],
        out_specs=pl.BlockSpec((bm, D), lambda i: (i, 0)),
        out_shape=jax.ShapeDtypeStruct((nrows, D), F32),
        compiler_params=_params("parallel"), name="norm_last")(h, y, gy.reshape(1, D))


def _mm_body(x_ref, w_ref, *refs, relu2, scale, nside):
    side_in, o_refs, side_out = refs[:nside], refs[nside:len(refs) - nside], refs[len(refs) - nside:]
    acc = jnp.dot(x_ref[...], w_ref[...], preferred_element_type=F32)
    if relu2:
        acc = jnp.square(jnp.maximum(acc, 0.0))
    if scale is not None:
        acc = acc * scale
    for o in o_refs:
        o[...] = acc.astype(o.dtype)
    for s_in, s_out in zip(side_in, side_out):
        s_out[...] = s_in[...].astype(s_out.dtype)


def _matmul(x, w, layer, out_dtypes, *, name, relu2=False, scale=None, col0=0, ncols=None, bm=1024, bn=1024,
            side=()):
    M, K = x.shape
    N = w.shape[2] - col0 if ncols is None else ncols
    bm = _pick(M, bm)
    bn = _pick(N, bn, LANES)
    assert col0 % bn == 0
    c0 = col0 // bn
    nj = N // bn
    steps = (M // bm) * nj
    in_specs = [pl.BlockSpec((bm, K), lambda i, j: (i, 0)),
                pl.BlockSpec((None, K, bn), lambda i, j: (layer, 0, c0 + j))]
    out_specs = [pl.BlockSpec((bm, bn), lambda i, j: (i, j)) for _ in out_dtypes]
    out_shape = [jax.ShapeDtypeStruct((M, N), dt) for dt in out_dtypes]
    args = [x, w]
    for src, src_layer in side:
        _, R, C = src.shape
        rb = next(r for r in range(BF16_ROWS, R + 1, BF16_ROWS) if R % r == 0 and R // r <= steps)
        last = R // rb - 1
        in_specs.append(pl.BlockSpec(
            (None, rb, C), lambda i, j, sl=src_layer, last=last: (sl, jnp.minimum(i * nj + j, last), 0)))
        out_specs.append(pl.BlockSpec(
            (None, rb, C), lambda i, j, last=last: (0, jnp.minimum(i * nj + j, last), 0)))
        out_shape.append(jax.ShapeDtypeStruct((1, R, C), BF16))
        args.append(src)
    sem = ("arbitrary", "arbitrary") if side else ("parallel", "arbitrary")
    return pl.pallas_call(
        functools.partial(_mm_body, relu2=relu2, scale=scale, nside=len(side)), grid=(M // bm, nj),
        in_specs=in_specs, out_specs=out_specs, out_shape=out_shape,
        compiler_params=_params(*sem), name=name)(*args)


def _mm_heads_body(x_ref, w_ref, *refs, hd):
    o32_ref, o16_ref = refs[-2:]
    acc = jnp.dot(x_ref[...], w_ref[...], preferred_element_type=F32)
    o16_ref[...] = acc.astype(o16_ref.dtype)
    o32_ref[...] = pltpu.einshape("m(hd)->mhd", acc, d=hd)


def _matmul_heads(x, w, prev32, prev16, *, name, col0, ncols, row0, nrows, layer, nlayers, hd, bm=512, bn=1024):
    M, K = x.shape
    bm = _pick(math.gcd(row0, nrows) if row0 else nrows, bm)
    bn = _pick(ncols, bn, LANES * SUBLANES)
    assert col0 % bn == 0 and bn % hd == 0
    c0, r0, hb = col0 // bn, row0 // bm, bn // hd
    in_specs = [pl.BlockSpec((bm, K), lambda i, j: (r0 + i, 0)),
                pl.BlockSpec((None, K, bn), lambda i, j: (0, 0, c0 + j))]
    args = [x, w]
    aliases = {}
    for out_idx, prev in enumerate((prev32, prev16)):
        if prev is not None:
            aliases[len(args)] = out_idx
            in_specs.append(pl.BlockSpec(memory_space=pl.ANY))
            args.append(prev)
    return pl.pallas_call(
        functools.partial(_mm_heads_body, hd=hd), grid=(nrows // bm, ncols // bn), in_specs=in_specs,
        out_specs=[pl.BlockSpec((None, bm, hb, hd), lambda i, j: (layer, i, j, 0)),
                   pl.BlockSpec((bm, bn), lambda i, j: (r0 + i, j))],
        out_shape=[jax.ShapeDtypeStruct((nlayers, nrows, ncols // hd, hd), F32),
                   jax.ShapeDtypeStruct((M, ncols), BF16)],
        input_output_aliases=aliases,
        compiler_params=_params("parallel", "arbitrary"), name=name)(*args)


def _mm_cat_body(*refs, splits):
    x_refs, w_ref, o_ref = refs[:-2], refs[-2], refs[-1]
    acc = None
    off = 0
    for x_ref, kk in zip(x_refs, splits):
        part = jnp.dot(x_ref[...], w_ref[off:off + kk, :], preferred_element_type=F32)
        acc = part if acc is None else acc + part
        off += kk
    o_ref[...] = acc


def _matmul_cat(xs, w, layer, *, name, bm=1024, bn=1024):
    M = xs[0].shape[0]
    _, K, N = w.shape
    splits = tuple(x.shape[1] for x in xs)
    assert sum(splits) == K
    bm = _pick(M, bm)
    bn = _pick(N, bn, LANES)
    in_specs = [pl.BlockSpec((bm, kk), lambda i, j: (i, 0)) for kk in splits]
    in_specs.append(pl.BlockSpec((None, K, bn), lambda i, j: (layer, 0, j)))
    return pl.pallas_call(
        functools.partial(_mm_cat_body, splits=splits), grid=(M // bm, N // bn),
        in_specs=in_specs, out_specs=pl.BlockSpec((bm, bn), lambda i, j: (i, j)),
        out_shape=jax.ShapeDtypeStruct((M, N), F32),
        compiler_params=_params("parallel", "arbitrary"), name=name)(*xs, w)


def _mm_kacc_body(x_ref, w_ref, o_ref, *, bn):
    k = pl.program_id(1)

    @pl.when(k == 0)
    def _():
        o_ref[...] = jnp.zeros(o_ref.shape, F32)

    x = x_ref[...]
    for n0 in range(0, o_ref.shape[1], bn):
        o_ref[:, n0:n0 + bn] += jnp.dot(x, w_ref[:, n0:n0 + bn], preferred_element_type=F32)


def _matmul_kacc(x, w, layer, *, name, bm=512, bk=2048):
    M, K = x.shape
    N = w.shape[2]
    bm = _pick(M, bm)
    bk = _pick(K, bk, LANES)
    return pl.pallas_call(
        functools.partial(_mm_kacc_body, bn=_pick(N, 512, LANES)), grid=(M // bm, K // bk),
        in_specs=[pl.BlockSpec((bm, bk), lambda i, k: (i, k)),
                  pl.BlockSpec((None, bk, N), lambda i, k: (layer, k, 0))],
        out_specs=pl.BlockSpec((bm, N), lambda i, k: (i, 0)),
        out_shape=jax.ShapeDtypeStruct((M, N), F32),
        compiler_params=_params("parallel", "arbitrary"), name=name)(x, w)


def _gates_body(t_ref, bf_ref, w2_ref, bg_ref, logf_ref, la_ref):
    t = t_ref[...]
    logf_ref[...] = _log_sigmoid(t + bf_ref[...])
    zg = jnp.dot(t, w2_ref[...], preferred_element_type=F32) + bg_ref[...]
    la_ref[...] = _log_sigmoid(zg) * (1.0 / GLA_TAU)


def _gates(tail, bfox_pad, w2_pad, bgla):
    M = tail.shape[0]
    KW = w2_pad.shape[1]
    bm = _pick(M, 512)
    return pl.pallas_call(
        _gates_body, grid=(M // bm,),
        in_specs=[pl.BlockSpec((bm, LANES), lambda i: (i, 0)),
                  pl.BlockSpec((1, LANES), lambda i: (0, 0)),
                  pl.BlockSpec((LANES, KW), lambda i: (0, 0)),
                  pl.BlockSpec((1, KW), lambda i: (0, 0))],
        out_specs=[pl.BlockSpec((bm, LANES), lambda i: (i, 0)),
                   pl.BlockSpec((bm, KW), lambda i: (i, 0))],
        out_shape=[jax.ShapeDtypeStruct((M, LANES), F32), jax.ShapeDtypeStruct((M, KW), F32)],
        compiler_params=_params("parallel"), name="gates")(tail, bfox_pad, w2_pad, bgla)


def _row_cumsum(x):
    n = x.shape[0]
    row = lax.broadcasted_iota(jnp.int32, x.shape, 0)
    sh = 1
    while sh < n:
        x = x + jnp.where(row >= sh, pltpu.roll(x, sh, axis=0), 0.0)
        sh *= 2
    return x


def _cumsum_body(x_ref, c_ref, carry_sc):
    @pl.when(pl.program_id(1) == 0)
    def _():
        carry_sc[...] = jnp.zeros(carry_sc.shape, F32)

    c = _row_cumsum(x_ref[...]) + carry_sc[...]
    c_ref[...] = c
    n = c.shape[0]
    carry_sc[...] = c[n - 1:n, :]


def _cumsum(x, bl):
    S, L, W = x.shape
    bl = _pick(L, bl)
    spec = pl.BlockSpec((None, bl, W), lambda s, i: (s, i, 0))
    return pl.pallas_call(
        _cumsum_body, grid=(S, L // bl), in_specs=[spec], out_specs=spec,
        out_shape=jax.ShapeDtypeStruct((S, L, W), F32),
        scratch_shapes=[pltpu.VMEM((1, W), F32)],
        compiler_params=_params("parallel", "arbitrary"), name="cumsum")(x)


_NT = (((1,), (1,)), ((), ()))
_TN = (((0,), (0,)), ((), ()))


def _fox_prompt_body(q_ref, k_ref, kb_ref, vt_ref, o_ref, acc_sc, st_sc, qa_sc, *, tq, tk, nsub):
    qi = pl.program_id(1)
    hd = vt_ref.shape[0]
    bq = nsub * tq
    nkb = bq // tk
    ones = jnp.ones((ONES_ROWS, tk), BF16)
    lane = lax.broadcasted_iota(jnp.int32, (bq, LANES), 1) - pl.program_id(0) * BIAS_TERMS
    qa_sc[:, :hd] = q_ref[...]
    qa_sc[:, hd:] = jnp.where(lane.astype(jnp.uint32) < BIAS_TERMS, 1.0, 0.0).astype(BF16)

    def mask(st, mask_off):
        key = lax.broadcasted_iota(jnp.int32, (tk, tq), 0) + mask_off
        qry = lax.broadcasted_iota(jnp.int32, (tk, tq), 1)
        return jnp.where(key <= qry, st, -jnp.inf)

    def scores(j, sub):
        s0 = pl.multiple_of(j * tk, tk)
        k = jnp.concatenate([k_ref[pl.ds(s0, tk), :], kb_ref[pl.ds(s0, tk), :]], axis=1)
        return lax.dot_general(k, qa_sc[sub * tq:(sub + 1) * tq, :], _NT,
                               preferred_element_type=F32)

    def accumulate(j, sub, st, m_blk, m_prev):
        s0 = pl.multiple_of(j * tk, tk)
        m_new = jnp.maximum(m_prev, m_blk)
        alpha = jnp.exp2(m_prev - m_new)
        pt = jnp.exp2(st - m_new).astype(BF16)
        vt = jnp.concatenate([vt_ref[:, pl.ds(s0, tk)], ones], axis=0)
        acc_sc[sub] = alpha * acc_sc[sub] + jnp.dot(vt, pt, preferred_element_type=F32)
        return m_new

    def issue(j, slot):
        mb = []
        for sub in range(nsub):
            st = scores(j, sub)
            st_sc[slot, sub] = st
            mb.append(jnp.max(st, axis=0, keepdims=True))
        return tuple(mb)

    acc_sc[...] = jnp.zeros(acc_sc.shape, F32)
    m0 = tuple(jnp.full((1, tq), -jnp.inf, F32) for _ in range(nsub))
    n = nkb * qi

    def body(jj, carry):
        ms, mb = carry
        for u in range(nkb):
            slot = u % 2
            j = nkb * jj + u
            mb_next = issue(j + 1, 1 - slot)
            ms = tuple(accumulate(j, sub, st_sc[slot, sub], mb[sub], ms[sub]) for sub in range(nsub))
            mb = mb_next
        return ms, mb

    assert nkb % 2 == 0
    ms, _ = lax.fori_loop(0, qi, body, (m0, issue(0, 0)))
    ms = list(ms)
    band = [(d, sub) for d in range(nkb) for sub in range(nsub) if d * tk <= (sub + 1) * tq - 1]
    late = {(d, sub): scores(n + d, sub) for d, sub in band if d > 0}
    for d, sub in band:
        k_lo, k_hi, q_lo = d * tk, (d + 1) * tk - 1, sub * tq
        st = st_sc[0, sub] if d == 0 else late[d, sub]
        if k_hi > q_lo:
            st = mask(st, k_lo - q_lo)
        ms[sub] = accumulate(n + d, sub, st, jnp.max(st, axis=0, keepdims=True), ms[sub])
    for sub in range(nsub):
        acc = acc_sc[sub]
        o = acc[:hd, :] / acc[hd:hd + 1, :]
        o_ref[sub * tq:(sub + 1) * tq, :] = jnp.transpose(o).astype(o_ref.dtype)


FOX_TQ, FOX_TK, FOX_NSUB = 512, 512, 4


def _fox_prompt(q16, k16, kb16, vt16, T):
    M, W = q16.shape
    H = W // FOX_HD
    tq, tk, nsub = FOX_TQ, FOX_TK, FOX_NSUB
    bq = nsub * tq
    assert T % bq == 0 and bq % tk == 0 and FOX_HD == LANES and H * BIAS_TERMS <= LANES
    return pl.pallas_call(
        functools.partial(_fox_prompt_body, tq=tq, tk=tk, nsub=nsub), grid=(H, T // bq),
        in_specs=[pl.BlockSpec((bq, FOX_HD), lambda h, i: (i, h)),
                  pl.BlockSpec((T, FOX_HD), lambda h, i: (0, h)),
                  pl.BlockSpec((T, LANES), lambda h, i: (0, 0)),
                  pl.BlockSpec((FOX_HD, T), lambda h, i: (h, 0))],
        out_specs=pl.BlockSpec((bq, FOX_HD), lambda h, i: (i, h)),
        out_shape=jax.ShapeDtypeStruct((M, W), BF16),
        scratch_shapes=[pltpu.VMEM((nsub, FOX_HD + ONES_ROWS, tq), F32),
                        pltpu.VMEM((2, nsub, tk, tq), F32),
                        pltpu.VMEM((bq, FOX_HD + LANES), BF16)],
        compiler_params=_params("parallel", "arbitrary"), name="fox_prompt")(q16, k16, kb16, vt16)


def _split_bf16(x, terms):
    parts = []
    for _ in range(terms):
        top = lax.bitcast_convert_type(lax.bitcast_convert_type(x, jnp.uint32) & jnp.uint32(0xFFFF0000), F32)
        parts.append(top.astype(BF16))
        x = x - top
    return jnp.stack(parts, axis=-1)


def _fox_sample_body(q_ref, kn_ref, vn_ref, kp_ref, vp_ref, c_ref, fo_any, o_ref, *, P, ST, HB):
    del fo_any
    r = lax.broadcasted_iota(jnp.int32, (ST, ST), 0)
    cc = lax.broadcasted_iota(jnp.int32, (ST, ST), 1)
    hd = FOX_HD
    kp = pltpu.einshape("phd->hpd", kp_ref[...]).astype(BF16)
    vp = pltpu.einshape("phd->hpd", vp_ref[...]).astype(BF16)
    scores = []
    for h in range(HB):
        sl = slice(h * hd, (h + 1) * hd)
        q = q_ref[:, sl]
        c = c_ref[h]
        cb = c[:, P - 1:P]
        sp = (lax.dot_general(q, kp[h], _NT, preferred_element_type=F32)
              + (cb - c[:, :P]) * LOG2E)
        sn = (lax.dot_general(q, kn_ref[:, sl], _NT, preferred_element_type=F32)
              + (cb - c[:, P:]) * LOG2E)
        scores.append((sp, jnp.where(cc <= r, sn, -jnp.inf)))
    for h, (sp, sn) in enumerate(scores):
        sl = slice(h * hd, (h + 1) * hd)
        m = jnp.maximum(jnp.max(sp, axis=1, keepdims=True), jnp.max(sn, axis=1, keepdims=True))
        pp = jnp.exp2(sp - m)
        pn = jnp.exp2(sn - m)
        l = jnp.sum(pp, axis=1, keepdims=True) + jnp.sum(pn, axis=1, keepdims=True)
        o = (jnp.dot(pp.astype(BF16), vp[h], preferred_element_type=F32)
             + jnp.dot(pn.astype(BF16), vn_ref[:, sl], preferred_element_type=F32))
        o_ref[:, sl] = (o / l).astype(o_ref.dtype)


def _fox_sample(fo, q16, k16, v16, cache_k, cache_v, ct, layer, T, SB, ST):
    M, W = q16.shape
    H = W // FOX_HD
    P = cache_k.shape[2]
    r0 = T // ST
    HB = SUBLANES
    assert H % HB == 0
    new = pl.BlockSpec((ST, HB * FOX_HD), lambda b, h: (r0 + b, h))
    past = pl.BlockSpec((None, None, P, HB, FOX_HD), lambda b, h: (layer, b, 0, h, 0))
    return pl.pallas_call(
        functools.partial(_fox_sample_body, P=P, ST=ST, HB=HB), grid=(SB, H // HB),
        in_specs=[new, new, new, past, past,
                  pl.BlockSpec((None, HB, 1, P + ST), lambda b, h: (b, h, 0, 0)),
                  pl.BlockSpec(memory_space=pl.ANY)],
        out_specs=new,
        out_shape=jax.ShapeDtypeStruct((M, W), BF16),
        input_output_aliases={6: 0},
        compiler_params=_params("parallel", "arbitrary"), name="fox_sample")(
            q16, k16, v16, cache_k, cache_v, ct, fo)


def _gla_body(q_ref, k_ref, v_ref, r_ref, la_ref, s0_ref, g_ref, go_any, o_ref, sout_ref,
              S_sc, b_sc, q_sc, *, C, NH, DK, DV, nc):
    del go_any
    c = pl.program_id(1)

    @pl.when(c == 0)
    def _():
        S_sc[...] = s0_ref[...]

    b = _row_cumsum(la_ref[...])
    b_sc[...] = b
    q = q_ref[...] * (DK ** -0.5)
    q_sc[...] = q
    bl = b[C - 1:C, :]
    qe = (q * jnp.exp(b)).astype(BF16)
    kdec = (k_ref[...] * jnp.exp(bl - b)).astype(BF16)
    ebl = jnp.exp(bl)
    v16 = v_ref[...].astype(BF16)

    R = GLA_SUB
    nb = C // R
    k = k_ref[...]
    srow_c = lax.broadcasted_iota(jnp.int32, b.shape, 0)
    at_off = [None] * NH
    for i in range(1, nb):
        b_ref_i = b[R * i - 1:R * i, :]
        in_blk = (srow_c >> GLA_SUB_LOG2) == i
        qt = (q * jnp.exp(jnp.where(in_blk, b - b_ref_i, -jnp.inf))).astype(BF16)
        kt = (k * jnp.exp(jnp.where(srow_c < R * i, b_ref_i - b, -jnp.inf))).astype(BF16)
        for h in range(NH):
            sl = slice(h * DK, (h + 1) * DK)
            part = lax.dot_general(kt[:, sl], qt[:, sl], _NT, preferred_element_type=F32)
            at_off[h] = part if at_off[h] is None else at_off[h] + part

    srow = lax.broadcasted_iota(jnp.int32, (R, DK), 0)
    lane = lax.broadcasted_iota(jnp.int32, (R, C), 1)
    at_diag = [[] for _ in range(NH)]
    for i in range(nb):
        blk = slice(R * i, R * (i + 1))
        ats = [jnp.zeros((R, C), F32) for _ in range(NH)]
        for r in range(R):
            t = R * i + r
            mask = srow <= r
            lm = lane == t
            for h in range(NH):
                sl = slice(h * DK, (h + 1) * DK)
                e = jnp.exp(jnp.where(mask, b_sc[t:t + 1, sl] - b_sc[blk, sl], -jnp.inf))
                w = e * (q_sc[t:t + 1, sl] * k_ref[blk, sl])
                ats[h] = jnp.where(lm, jnp.sum(w, axis=1, keepdims=True), ats[h])
        for h in range(NH):
            at_diag[h].append(ats[h])
    ats = []
    for h in range(NH):
        at = jnp.concatenate(at_diag[h], axis=0)
        ats.append(at if at_off[h] is None else at + at_off[h])

    g = g_ref[...]
    for h in range(NH):
        sl = slice(h * DK, (h + 1) * DK)
        vs = slice(h * DV, (h + 1) * DV)
        S = S_sc[h]
        o = (jnp.dot(qe[:, sl], S.astype(BF16), preferred_element_type=F32)
             + lax.dot_general(ats[h].astype(BF16), v16[:, vs], _TN, preferred_element_type=F32))
        r = r_ref[:, vs]
        o = _rms(o) * g * (r * (1.0 / (1.0 + jnp.exp(-r))))
        o_ref[:, vs] = o.astype(o_ref.dtype)
        dm = jnp.transpose(jnp.broadcast_to(ebl[:, sl], (DK, DK)))
        decay = jnp.concatenate([dm] * (DV // DK), axis=1)
        S_sc[h] = decay * S + lax.dot_general(kdec[:, sl], v16[:, vs], _TN, preferred_element_type=F32)

    @pl.when(c == nc - 1)
    def _():
        sout_ref[...] = S_sc[...]


def _gla(rest, la, s0, g, go, *, row0, nseq, nc, C, DK, DV):
    M = rest.shape[0]
    NH = GLA_HEADS
    KW, VW = NH * DK, NH * DV
    r0 = row0 // C
    assert row0 % C == 0 and VW == 2 * KW

    def rows(col):
        return lambda s, c: (r0 + s * nc + c, col)

    in_specs = [pl.BlockSpec((C, KW), rows(0)), pl.BlockSpec((C, KW), rows(1)),
                pl.BlockSpec((C, VW), rows(1)), pl.BlockSpec((C, VW), rows(2)),
                pl.BlockSpec((C, KW), rows(0)),
                pl.BlockSpec((None, NH, DK, DV), lambda s, c: (s, 0, 0, 0)),
                pl.BlockSpec((1, DV), lambda s, c: (0, 0))]
    args = [rest, rest, rest, rest, la, s0, g.reshape(1, DV)]
    aliases = {}
    if go is not None:
        in_specs.append(pl.BlockSpec(memory_space=pl.ANY))
        args.append(go)
        aliases = {7: 0}
        body = _gla_body
    else:
        body = functools.partial(_gla_body_noalias)
    return pl.pallas_call(
        functools.partial(body, C=C, NH=NH, DK=DK, DV=DV, nc=nc), grid=(nseq, nc),
        in_specs=in_specs,
        out_specs=[pl.BlockSpec((C, VW), rows(0)),
                   pl.BlockSpec((None, NH, DK, DV), lambda s, c: (s, 0, 0, 0))],
        out_shape=[jax.ShapeDtypeStruct((M, VW), BF16),
                   jax.ShapeDtypeStruct((nseq, NH, DK, DV), F32)],
        scratch_shapes=[pltpu.VMEM((NH, DK, DV), F32), pltpu.VMEM((C, KW), F32),
                        pltpu.VMEM((C, KW), F32)],
        input_output_aliases=aliases,
        compiler_params=_params("parallel", "arbitrary"), name="gla")(*args)


def _gla_body_noalias(q_ref, k_ref, v_ref, r_ref, la_ref, s0_ref, g_ref, o_ref, sout_ref,
                      S_sc, b_sc, q_sc, **kw):
    _gla_body(q_ref, k_ref, v_ref, r_ref, la_ref, s0_ref, g_ref, None, o_ref, sout_ref,
              S_sc, b_sc, q_sc, **kw)


def _pool_body(*refs, bm, pos0, GC, aliased):
    if aliased:
        p_ref, hist_ref, w_ref, sp_ref, _, o_ref, hout_ref, xa_sc = refs
    else:
        p_ref, hist_ref, w_ref, sp_ref, o_ref, hout_ref, xa_sc = refs
    i = pl.program_id(1)

    @pl.when(i == 0)
    def _():
        xa_sc[0:POOL_CARRY, :] = hist_ref[...]

    xa_sc[POOL_CARRY:POOL_CARRY + bm, :] = p_ref[...]
    pos = pos0 + i * bm + lax.broadcasted_iota(jnp.int32, (bm, 1), 0)
    for gi, w in enumerate(POOL_WINDOWS):
        cs = slice(gi * GC, (gi + 1) * GC)
        x = xa_sc[POOL_CARRY:POOL_CARRY + bm, cs]
        tot = x
        for d in range(1, w):
            tot = tot + xa_sc[POOL_CARRY - d:POOL_CARRY - d + bm, cs]
        cnt = jnp.minimum(pos + 1, w).astype(F32)
        dd = (tot / cnt - x).astype(BF16)
        y = jnp.dot(dd, w_ref[gi], preferred_element_type=F32) * sp_ref[:, cs]
        o_ref[:, cs] = y.astype(o_ref.dtype)
    last = xa_sc[bm:bm + POOL_CARRY, :]
    hout_ref[...] = last
    xa_sc[0:POOL_CARRY, :] = last


def _pool(rest, col, hist, w16, sp, po, *, row0, nseq, nb, bm, pos0):
    M = rest.shape[0]
    G, GC, _ = w16.shape
    PW = G * GC
    r0 = row0 // bm
    assert row0 % bm == 0 and bm >= POOL_CARRY
    in_specs = [pl.BlockSpec((bm, PW), lambda s, i: (r0 + s * nb + i, col)),
                pl.BlockSpec((None, POOL_CARRY, PW), lambda s, i: (s, 0, 0)),
                pl.BlockSpec((G, GC, GC), lambda s, i: (0, 0, 0)),
                pl.BlockSpec((1, PW), lambda s, i: (0, 0))]
    args = [rest, hist, w16, sp.reshape(1, PW)]
    aliases = {}
    if po is not None:
        in_specs.append(pl.BlockSpec(memory_space=pl.ANY))
        args.append(po)
        aliases = {4: 0}
    return pl.pallas_call(
        functools.partial(_pool_body, bm=bm, pos0=pos0, GC=GC, aliased=po is not None),
        grid=(nseq, nb), in_specs=in_specs,
        out_specs=[pl.BlockSpec((bm, PW), lambda s, i: (r0 + s * nb + i, 0)),
                   pl.BlockSpec((None, POOL_CARRY, PW), lambda s, i: (s, 0, 0))],
        out_shape=[jax.ShapeDtypeStruct((M, PW), BF16),
                   jax.ShapeDtypeStruct((nseq, POOL_CARRY, PW), F32)],
        scratch_shapes=[pltpu.VMEM((POOL_CARRY + bm, PW), F32)],
        input_output_aliases=aliases,
        compiler_params=_params("parallel", "arbitrary"), name="pool")(*args)


def kernel(x_prompt, x_sample, cache_fox_k, cache_fox_v, cache_fox_logf, state_gla, state_pool, g_pre_mix, w_in, b_fox_f, w_gla_a2, b_gla_a, g_gla_norm, w_pool, s_pool, w_out, g_post_mix, g_pre_mlp, w_up, w_down, g_post_mlp):
    B, T, D = x_prompt.shape
    SB, ST, _ = x_sample.shape
    L = w_in.shape[0]
    P = cache_fox_k.shape[2]
    assert B == 1
    FW = D // 2
    H = FW // FOX_HD
    GW = D // 4
    DV = GW // GLA_HEADS
    DK = DV // 2
    KW = GLA_HEADS * DK
    RANK = w_gla_a2.shape[1]
    PW = D - FW - GW
    GC = PW // len(POOL_WINDOWS)
    MS = SB * ST
    M = T + MS
    assert H + RANK <= LANES and 2 * KW + 2 * GW == 3 * PW and PW % LANES == 0

    o_fq, o_fk, o_fv, o_ff = 0, FW, 2 * FW, 3 * FW
    o_gq = o_ff + H
    o_ga = o_gq + 2 * KW + GW
    o_gr = o_ga + RANK

    w_in16, w_out16, w_up16 = (w[:1].astype(BF16) for w in (w_in, w_out, w_up))
    w_pool16 = w_pool.astype(BF16)
    past_logf = jnp.pad(cache_fox_logf.astype(F32), ((0, 0), (0, 0), (0, 0), (0, LANES - H)))
    zero_state = jnp.zeros((1, GLA_HEADS, DK, DV), F32)
    zero_hist = jnp.zeros((1, POOL_CARRY, PW), F32)
    sample_hist = jnp.pad(state_pool, ((0, 0), (0, 0), (POOL_CARRY - POOL_HIST, 0), (0, 0)))

    h = None
    x_groups = ((x_prompt.reshape(T, D), 0), (x_sample.reshape(MS, D), T))
    a = None
    for xg, row0 in x_groups:
        a = _norm_first(xg, g_pre_mix[0], a, row0, M)
    outs_p = [[] for _ in range(5)]
    outs_s = [[] for _ in range(5)]
    kv_p, kv_s = [None, None], [None, None]
    for l in range(L):
        bfox_pad = jnp.pad(b_fox_f[l].astype(F32), (0, LANES - H)).reshape(1, LANES)
        w2_pad = jnp.pad(w_gla_a2[l].astype(F32), ((H, LANES - H - RANK), (0, 0)))
        w_rest16 = jnp.concatenate([w_in16[:, :, o_gq:o_ga], w_in16[:, :, o_gr:]], axis=2)
        w_tail16 = jnp.concatenate([w_in16[:, :, o_ff:o_gq], w_in16[:, :, o_ga:o_gr],
                                    jnp.zeros((1, D, LANES - H - RANK), BF16)], axis=2)

        (q16,) = _matmul(a, w_in16, 0, [BF16], name="proj_q", col0=o_fq, ncols=FW,
                         scale=FOX_HD ** -0.5 * LOG2E)
        kv16 = []
        for t, (name, col0) in enumerate((("proj_k", o_fk), ("proj_v", o_fv))):
            common = dict(name=name, col0=col0, ncols=FW, layer=l, nlayers=L, hd=FOX_HD)
            kv_p[t], x16 = _matmul_heads(a, w_in16, kv_p[t], None, row0=0, nrows=T, **common)
            kv_s[t], x16 = _matmul_heads(a, w_in16, kv_s[t], x16, row0=T, nrows=MS, **common)
            kv16.append(x16)
        k16, v16 = kv16
        (rest,) = _matmul(a, w_rest16, 0, [F32], name="proj_rest")
        (tail,) = _matmul(a, w_tail16, 0, [F32], name="proj_tail")
        logf, la = _gates(tail, bfox_pad, w2_pad, b_gla_a[l].reshape(1, KW))

        c_p = _cumsum(logf[:T].reshape(1, T, LANES), 1024)
        bias = _split_bf16(c_p[0, :, :H] * (-LOG2E), BIAS_TERMS)
        kb16 = jnp.pad(bias.reshape(T, H * BIAS_TERMS), ((0, 0), (0, LANES - H * BIAS_TERMS)))
        vt16 = jnp.transpose(v16[:T])
        fo = _fox_prompt(q16, k16, kb16, vt16, T)
        lf_all = jnp.concatenate([past_logf[l], logf[T:].reshape(SB, ST, LANES)], axis=1)
        c_s = _cumsum(lf_all, P + ST)
        ct_s = jnp.transpose(c_s[:, :, :H], (0, 2, 1)).reshape(SB, H, 1, P + ST)
        fo = _fox_sample(fo, q16, k16, v16, cache_fox_k, cache_fox_v, ct_s, l, T, SB, ST)

        go, gla_p = _gla(rest, la, zero_state, g_gla_norm[l], None,
                         row0=0, nseq=1, nc=T // GLA_CHUNK, C=GLA_CHUNK, DK=DK, DV=DV)
        go, gla_s = _gla(rest, la, state_gla[l], g_gla_norm[l], go,
                         row0=T, nseq=SB, nc=1, C=ST, DK=DK, DV=DV)

        pcol = (2 * KW + 2 * GW) // PW
        bp = _pick(T, 512)
        po, hist_p = _pool(rest, pcol, zero_hist, w_pool16[l], s_pool[l], None,
                           row0=0, nseq=1, nb=T // bp, bm=bp, pos0=0)
        po, hist_s = _pool(rest, pcol, sample_hist[l], w_pool16[l], s_pool[l], po,
                           row0=T, nseq=SB, nb=1, bm=ST, pos0=P)

        y = _matmul_cat([fo, go, po], w_out16, 0, name="proj_out")
        if h is None:
            hc = None
            for xg, row0 in x_groups:
                hc = _norm_mid(xg, y, g_post_mix[l], g_pre_mlp[l], hc, row0)
            h, c = hc
        else:
            h, c = _norm_mid(h, y, g_post_mix[l], g_pre_mlp[l])
        later = [(w, l + 1) for w in (w_up, w_in, w_out)] if l + 1 < L else []
        hid, w_down16, *nxt = _matmul(c, w_up16, 0, [BF16], name="mlp_up", relu2=True,
                                      side=[(w_down, l)] + later)
        f = _matmul_kacc(hid, w_down16, 0, name="mlp_down")
        if nxt:
            w_up16, w_in16, w_out16 = nxt
        if l + 1 < L:
            h, a = _norm_mid(h, f, g_post_mlp[l], g_pre_mix[l + 1])
        else:
            y_prompt = _norm_last(h, f, g_post_mlp[l], 0, T)
            y_sample = _norm_last(h, f, g_post_mlp[l], T, MS)

        outs_p[2].append(logf[:T, :H].reshape(1, T, H))
        outs_p[3].append(gla_p)
        outs_p[4].append(hist_p[:, POOL_CARRY - POOL_HIST:])
        outs_s[2].append(logf[T:, :H].reshape(SB, ST, H))
        outs_s[3].append(gla_s)
        outs_s[4].append(hist_s[:, POOL_CARRY - POOL_HIST:])

    p_kv = [x.reshape(L, 1, T, H, FOX_HD) for x in kv_p]
    s_kv = [x.reshape(L, SB, ST, H, FOX_HD) for x in kv_s]
    return (y_prompt.reshape(1, T, D), y_sample.reshape(SB, ST, D),
            *p_kv, *[jnp.stack(o) for o in outs_p[2:]], *s_kv, *[jnp.stack(o) for o in outs_s[2:]])
```

```python
import functools
import math

import jax
import jax.numpy as jnp
from jax import lax
from jax.experimental import pallas as pl
from jax.experimental.pallas import tpu as pltpu

F32 = jnp.float32
BF16 = jnp.bfloat16

RMS_EPS = 1e-6
FOX_HD = 128
GLA_HEADS = 4
GLA_TAU = 16.0
GLA_CHUNK = 64
GLA_SUB_LOG2 = 4
GLA_SUB = 1 << GLA_SUB_LOG2
POOL_WINDOWS = (2, 4, 8, 16)
POOL_HIST = 15
POOL_CARRY = 16
LANES = 128
SUBLANES = 8
BF16_ROWS = 16
ONES_ROWS = BF16_ROWS
LOG2E = 1.4426950408889634
BIAS_TERMS = 3
VMEM_LIMIT = 56 * 1024 * 1024
CAST_BLOCK_BYTES = 4 * 1024 * 1024


def _params(*sem):
    return pltpu.CompilerParams(dimension_semantics=sem, vmem_limit_bytes=VMEM_LIMIT)


def _pick(n, pref, mult=8):
    b = min(pref, n)
    b -= b % mult
    while b > 0 and n % b:
        b -= mult
    assert b > 0, (n, pref, mult)
    return b


def _rms(x):
    return x * lax.rsqrt(jnp.mean(x * x, axis=-1, keepdims=True) + RMS_EPS)


def _log_sigmoid(x):
    return jnp.minimum(x, 0.0) - jnp.log1p(jnp.exp(-jnp.abs(x)))


def _norm_first_body(h_ref, g_ref, a_ref):
    a_ref[...] = (_rms(h_ref[...]) * g_ref[...]).astype(a_ref.dtype)


def _norm_mid_body(h_ref, y_ref, gy_ref, g_ref, h_out, a_out):
    h = h_ref[...] + _rms(y_ref[...]) * gy_ref[...]
    h_out[...] = h
    a_out[...] = (_rms(h) * g_ref[...]).astype(a_out.dtype)


def _norm_last_body(h_ref, y_ref, gy_ref, h_out):
    h_out[...] = h_ref[...] + _rms(y_ref[...]) * gy_ref[...]


def _row_block(row0, nrows, pref):
    return _pick(math.gcd(row0, nrows) if row0 else nrows, pref)


def _norm_first(x, g, prev, row0, M):
    nrows, D = x.shape
    bm = _row_block(row0, nrows, 256)
    r0 = row0 // bm
    in_specs = [pl.BlockSpec((bm, D), lambda i: (i, 0)), pl.BlockSpec((1, D), lambda i: (0, 0))]
    args = [x, g.reshape(1, D)]
    if prev is not None:
        in_specs.append(pl.BlockSpec(memory_space=pl.ANY))
        args.append(prev)
    return pl.pallas_call(
        lambda x_ref, g_ref, *rest: _norm_first_body(x_ref, g_ref, rest[-1]),
        grid=(nrows // bm,), in_specs=in_specs, out_specs=pl.BlockSpec((bm, D), lambda i: (r0 + i, 0)),
        out_shape=jax.ShapeDtypeStruct((M, D), BF16),
        input_output_aliases={2: 0} if prev is not None else {},
        compiler_params=_params("parallel"), name="norm_first")(*args)


def _norm_mid(h_src, y, gy, g, prev=None, row0=0):
    nrows, D = h_src.shape
    M = y.shape[0]
    bm = _row_block(row0, nrows, 256)
    r0 = row0 // bm
    src = pl.BlockSpec((bm, D), lambda i: (i, 0))
    row = pl.BlockSpec((bm, D), lambda i: (r0 + i, 0))
    vec = pl.BlockSpec((1, D), lambda i: (0, 0))
    in_specs = [src, row, vec, vec]
    args = [h_src, y, gy.reshape(1, D), g.reshape(1, D)]
    if prev is not None:
        in_specs += [pl.BlockSpec(memory_space=pl.ANY)] * 2
        args += list(prev)
    return pl.pallas_call(
        lambda h_ref, y_ref, gy_ref, g_ref, *rest: _norm_mid_body(h_ref, y_ref, gy_ref, g_ref, *rest[-2:]),
        grid=(nrows // bm,), in_specs=in_specs, out_specs=[row, row],
        out_shape=[jax.ShapeDtypeStruct((M, D), F32), jax.ShapeDtypeStruct((M, D), BF16)],
        input_output_aliases={4: 0, 5: 1} if prev is not None else {},
        compiler_params=_params("parallel"), name="norm_mid")(*args)


def _norm_last(h, y, gy, row0, nrows):
    D = h.shape[1]
    bm = _pick(math.gcd(row0, nrows) if row0 else nrows, 256)
    r0 = row0 // bm
    row = pl.BlockSpec((bm, D), lambda i: (r0 + i, 0))
    vec = pl.BlockSpec((1, D), lambda i: (0, 0))
    return pl.pallas_call(
        _norm_last_body, grid=(nrows // bm,), in_specs=[row, row, vec],
        out_specs=pl.BlockSpec((bm, D), lambda i: (i, 0)),
        out_shape=jax.ShapeDtypeStruct((nrows, D), F32),
        compiler_params=_params("parallel"), name="norm_last")(h, y, gy.reshape(1, D))


def _mm_body(x_ref, w_ref, *refs, relu2, scale, nside):
    side_in, o_refs, side_out = refs[:nside], refs[nside:len(refs) - nside], refs[len(refs) - nside:]
    acc = jnp.dot(x_ref[...], w_ref[...], preferred_element_type=F32)
    if relu2:
        acc = jnp.square(jnp.maximum(acc, 0.0))
    if scale is not None:
        acc = acc * scale
    for o in o_refs:
        o[...] = acc.astype(o.dtype)
    for s_in, s_out in zip(side_in, side_out):
        s_out[...] = s_in[...].astype(s_out.dtype)


def _matmul(x, w, layer, out_dtypes, *, name, relu2=False, scale=None, col0=0, ncols=None, bm=1024, bn=1024,
            side=()):
    M, K = x.shape
    N = w.shape[2] - col0 if ncols is None else ncols
    bm = _pick(M, bm)
    bn = _pick(N, bn, LANES)
    assert col0 % bn == 0
    c0 = col0 // bn
    nj = N // bn
    steps = (M // bm) * nj
    in_specs = [pl.BlockSpec((bm, K), lambda i, j: (i, 0)),
                pl.BlockSpec((None, K, bn), lambda i, j: (layer, 0, c0 + j))]
    out_specs = [pl.BlockSpec((bm, bn), lambda i, j: (i, j)) for _ in out_dtypes]
    out_shape = [jax.ShapeDtypeStruct((M, N), dt) for dt in out_dtypes]
    args = [x, w]
    for src, src_layer in side:
        _, R, C = src.shape
        rb = next(r for r in range(BF16_ROWS, R + 1, BF16_ROWS) if R % r == 0 and R // r <= steps)
        last = R // rb - 1
        in_specs.append(pl.BlockSpec(
            (None, rb, C), lambda i, j, sl=src_layer, last=last: (sl, jnp.minimum(i * nj + j, last), 0)))
        out_specs.append(pl.BlockSpec(
            (None, rb, C), lambda i, j, last=last: (0, jnp.minimum(i * nj + j, last), 0)))
        out_shape.append(jax.ShapeDtypeStruct((1, R, C), BF16))
        args.append(src)
    sem = ("arbitrary", "arbitrary") if side else ("parallel", "arbitrary")
    return pl.pallas_call(
        functools.partial(_mm_body, relu2=relu2, scale=scale, nside=len(side)), grid=(M // bm, nj),
        in_specs=in_specs, out_specs=out_specs, out_shape=out_shape,
        compiler_params=_params(*sem), name=name)(*args)


def _cast_body(w_ref, o_ref):
    o_ref[...] = w_ref[...].astype(o_ref.dtype)


def _cast_layer(w, layer):
    _, R, C = w.shape
    rb = _pick(R, max(BF16_ROWS, CAST_BLOCK_BYTES // (4 * C)), BF16_ROWS)
    return pl.pallas_call(
        _cast_body, grid=(R // rb,),
        in_specs=[pl.BlockSpec((None, rb, C), lambda i: (layer, i, 0))],
        out_specs=pl.BlockSpec((None, rb, C), lambda i: (0, i, 0)),
        out_shape=jax.ShapeDtypeStruct((1, R, C), BF16),
        compiler_params=_params("parallel"), name="cast_weight")(w)


def _mm_heads_body(x_ref, w_ref, *refs, hd):
    o32_ref, o16_ref = refs[-2:]
    acc = jnp.dot(x_ref[...], w_ref[...], preferred_element_type=F32)
    o16_ref[...] = acc.astype(o16_ref.dtype)
    o32_ref[...] = pltpu.einshape("m(hd)->mhd", acc, d=hd)


def _matmul_heads(x, w, prev32, prev16, *, name, col0, ncols, row0, nrows, layer, nlayers, hd, bm=512, bn=1024):
    M, K = x.shape
    bm = _pick(math.gcd(row0, nrows) if row0 else nrows, bm)
    bn = _pick(ncols, bn, LANES * SUBLANES)
    assert col0 % bn == 0 and bn % hd == 0
    c0, r0, hb = col0 // bn, row0 // bm, bn // hd
    in_specs = [pl.BlockSpec((bm, K), lambda i, j: (r0 + i, 0)),
                pl.BlockSpec((None, K, bn), lambda i, j: (0, 0, c0 + j))]
    args = [x, w]
    aliases = {}
    for out_idx, prev in enumerate((prev32, prev16)):
        if prev is not None:
            aliases[len(args)] = out_idx
            in_specs.append(pl.BlockSpec(memory_space=pl.ANY))
            args.append(prev)
    return pl.pallas_call(
        functools.partial(_mm_heads_body, hd=hd), grid=(nrows // bm, ncols // bn), in_specs=in_specs,
        out_specs=[pl.BlockSpec((None, bm, hb, hd), lambda i, j: (layer, i, j, 0)),
                   pl.BlockSpec((bm, bn), lambda i, j: (r0 + i, j))],
        out_shape=[jax.ShapeDtypeStruct((nlayers, nrows, ncols // hd, hd), F32),
                   jax.ShapeDtypeStruct((M, ncols), BF16)],
        input_output_aliases=aliases,
        compiler_params=_params("parallel", "arbitrary"), name=name)(*args)


def _mm_cat_body(*refs, splits):
    x_refs, w_ref, o_ref = refs[:-2], refs[-2], refs[-1]
    acc = None
    off = 0
    for x_ref, kk in zip(x_refs, splits):
        part = jnp.dot(x_ref[...], w_ref[off:off + kk, :], preferred_element_type=F32)
        acc = part if acc is None else acc + part
        off += kk
    o_ref[...] = acc


def _matmul_cat(xs, w, layer, *, name, bm=1024, bn=1024):
    M = xs[0].shape[0]
    _, K, N = w.shape
    splits = tuple(x.shape[1] for x in xs)
    assert sum(splits) == K
    bm = _pick(M, bm)
    bn = _pick(N, bn, LANES)
    in_specs = [pl.BlockSpec((bm, kk), lambda i, j: (i, 0)) for kk in splits]
    in_specs.append(pl.BlockSpec((None, K, bn), lambda i, j: (layer, 0, j)))
    return pl.pallas_call(
        functools.partial(_mm_cat_body, splits=splits), grid=(M // bm, N // bn),
        in_specs=in_specs, out_specs=pl.BlockSpec((bm, bn), lambda i, j: (i, j)),
        out_shape=jax.ShapeDtypeStruct((M, N), F32),
        compiler_params=_params("parallel", "arbitrary"), name=name)(*xs, w)


def _mm_kacc_body(x_ref, w_ref, o_ref, *, bn):
    k = pl.program_id(1)

    @pl.when(k == 0)
    def _():
        o_ref[...] = jnp.zeros(o_ref.shape, F32)

    x = x_ref[...]
    for n0 in range(0, o_ref.shape[1], bn):
        o_ref[:, n0:n0 + bn] += jnp.dot(x, w_ref[:, n0:n0 + bn], preferred_element_type=F32)


def _matmul_kacc(x, w, layer, *, name, bm=512, bk=2048):
    M, K = x.shape
    N = w.shape[2]
    bm = _pick(M, bm)
    bk = _pick(K, bk, LANES)
    return pl.pallas_call(
        functools.partial(_mm_kacc_body, bn=_pick(N, 512, LANES)), grid=(M // bm, K // bk),
        in_specs=[pl.BlockSpec((bm, bk), lambda i, k: (i, k)),
                  pl.BlockSpec((None, bk, N), lambda i, k: (layer, k, 0))],
        out_specs=pl.BlockSpec((bm, N), lambda i, k: (i, 0)),
        out_shape=jax.ShapeDtypeStruct((M, N), F32),
        compiler_params=_params("parallel", "arbitrary"), name=name)(x, w)


def _gates_body(t_ref, bf_ref, w2_ref, bg_ref, logf_ref, la_ref):
    t = t_ref[...]
    logf_ref[...] = _log_sigmoid(t + bf_ref[...])
    zg = jnp.dot(t, w2_ref[...], preferred_element_type=F32) + bg_ref[...]
    la_ref[...] = _log_sigmoid(zg) * (1.0 / GLA_TAU)


def _gates(tail, bfox_pad, w2_pad, bgla):
    M = tail.shape[0]
    KW = w2_pad.shape[1]
    bm = _pick(M, 512)
    return pl.pallas_call(
        _gates_body, grid=(M // bm,),
        in_specs=[pl.BlockSpec((bm, LANES), lambda i: (i, 0)),
                  pl.BlockSpec((1, LANES), lambda i: (0, 0)),
                  pl.BlockSpec((LANES, KW), lambda i: (0, 0)),
                  pl.BlockSpec((1, KW), lambda i: (0, 0))],
        out_specs=[pl.BlockSpec((bm, LANES), lambda i: (i, 0)),
                   pl.BlockSpec((bm, KW), lambda i: (i, 0))],
        out_shape=[jax.ShapeDtypeStruct((M, LANES), F32), jax.ShapeDtypeStruct((M, KW), F32)],
        compiler_params=_params("parallel"), name="gates")(tail, bfox_pad, w2_pad, bgla)


def _row_cumsum(x):
    n = x.shape[0]
    row = lax.broadcasted_iota(jnp.int32, x.shape, 0)
    sh = 1
    while sh < n:
        x = x + jnp.where(row >= sh, pltpu.roll(x, sh, axis=0), 0.0)
        sh *= 2
    return x


def _cumsum_body(x_ref, c_ref, carry_sc):
    @pl.when(pl.program_id(1) == 0)
    def _():
        carry_sc[...] = jnp.zeros(carry_sc.shape, F32)

    c = _row_cumsum(x_ref[...]) + carry_sc[...]
    c_ref[...] = c
    n = c.shape[0]
    carry_sc[...] = c[n - 1:n, :]


def _cumsum(x, bl):
    S, L, W = x.shape
    bl = _pick(L, bl)
    spec = pl.BlockSpec((None, bl, W), lambda s, i: (s, i, 0))
    return pl.pallas_call(
        _cumsum_body, grid=(S, L // bl), in_specs=[spec], out_specs=spec,
        out_shape=jax.ShapeDtypeStruct((S, L, W), F32),
        scratch_shapes=[pltpu.VMEM((1, W), F32)],
        compiler_params=_params("parallel", "arbitrary"), name="cumsum")(x)


_NT = (((1,), (1,)), ((), ()))
_TN = (((0,), (0,)), ((), ()))


def _fox_prompt_body(q_ref, k_ref, kb_ref, vt_ref, o_ref, acc_sc, st_sc, qa_sc, *, tq, tk, nsub):
    qi = pl.program_id(1)
    hd = vt_ref.shape[0]
    bq = nsub * tq
    nkb = bq // tk
    ones = jnp.ones((ONES_ROWS, tk), BF16)
    lane = lax.broadcasted_iota(jnp.int32, (bq, LANES), 1) - pl.program_id(0) * BIAS_TERMS
    qa_sc[:, :hd] = q_ref[...]
    qa_sc[:, hd:] = jnp.where(lane.astype(jnp.uint32) < BIAS_TERMS, 1.0, 0.0).astype(BF16)

    def mask(st, mask_off):
        key = lax.broadcasted_iota(jnp.int32, (tk, tq), 0) + mask_off
        qry = lax.broadcasted_iota(jnp.int32, (tk, tq), 1)
        return jnp.where(key <= qry, st, -jnp.inf)

    def scores(j, sub):
        s0 = pl.multiple_of(j * tk, tk)
        k = jnp.concatenate([k_ref[pl.ds(s0, tk), :], kb_ref[pl.ds(s0, tk), :]], axis=1)
        return lax.dot_general(k, qa_sc[sub * tq:(sub + 1) * tq, :], _NT,
                               preferred_element_type=F32)

    def accumulate(j, sub, st, m_blk, m_prev):
        s0 = pl.multiple_of(j * tk, tk)
        m_new = jnp.maximum(m_prev, m_blk)
        alpha = jnp.exp2(m_prev - m_new)
        pt = jnp.exp2(st - m_new).astype(BF16)
        vt = jnp.concatenate([vt_ref[:, pl.ds(s0, tk)], ones], axis=0)
        acc_sc[sub] = alpha * acc_sc[sub] + jnp.dot(vt, pt, preferred_element_type=F32)
        return m_new

    def issue(j, slot):
        mb = []
        for sub in range(nsub):
            st = scores(j, sub)
            st_sc[slot, sub] = st
            mb.append(jnp.max(st, axis=0, keepdims=True))
        return tuple(mb)

    acc_sc[...] = jnp.zeros(acc_sc.shape, F32)
    m0 = tuple(jnp.full((1, tq), -jnp.inf, F32) for _ in range(nsub))
    n = nkb * qi

    def body(jj, carry):
        ms, mb = carry
        for u in range(nkb):
            slot = u % 2
            j = nkb * jj + u
            mb_next = issue(j + 1, 1 - slot)
            ms = tuple(accumulate(j, sub, st_sc[slot, sub], mb[sub], ms[sub]) for sub in range(nsub))
            mb = mb_next
        return ms, mb

    assert nkb % 2 == 0
    ms, _ = lax.fori_loop(0, qi, body, (m0, issue(0, 0)))
    ms = list(ms)
    band = [(d, sub) for d in range(nkb) for sub in range(nsub) if d * tk <= (sub + 1) * tq - 1]
    late = {(d, sub): scores(n + d, sub) for d, sub in band if d > 0}
    for d, sub in band:
        k_lo, k_hi, q_lo = d * tk, (d + 1) * tk - 1, sub * tq
        st = st_sc[0, sub] if d == 0 else late[d, sub]
        if k_hi > q_lo:
            st = mask(st, k_lo - q_lo)
        ms[sub] = accumulate(n + d, sub, st, jnp.max(st, axis=0, keepdims=True), ms[sub])
    for sub in range(nsub):
        acc = acc_sc[sub]
        o = acc[:hd, :] / acc[hd:hd + 1, :]
        o_ref[sub * tq:(sub + 1) * tq, :] = jnp.transpose(o).astype(o_ref.dtype)


FOX_TQ, FOX_TK, FOX_NSUB = 512, 512, 4


def _fox_prompt(q16, k16, kb16, vt16, T):
    M, W = q16.shape
    H = W // FOX_HD
    tq, tk, nsub = FOX_TQ, FOX_TK, FOX_NSUB
    bq = nsub * tq
    assert T % bq == 0 and bq % tk == 0 and FOX_HD == LANES and H * BIAS_TERMS <= LANES
    return pl.pallas_call(
        functools.partial(_fox_prompt_body, tq=tq, tk=tk, nsub=nsub), grid=(H, T // bq),
        in_specs=[pl.BlockSpec((bq, FOX_HD), lambda h, i: (i, h)),
                  pl.BlockSpec((T, FOX_HD), lambda h, i: (0, h)),
                  pl.BlockSpec((T, LANES), lambda h, i: (0, 0)),
                  pl.BlockSpec((FOX_HD, T), lambda h, i: (h, 0))],
        out_specs=pl.BlockSpec((bq, FOX_HD), lambda h, i: (i, h)),
        out_shape=jax.ShapeDtypeStruct((M, W), BF16),
        scratch_shapes=[pltpu.VMEM((nsub, FOX_HD + ONES_ROWS, tq), F32),
                        pltpu.VMEM((2, nsub, tk, tq), F32),
                        pltpu.VMEM((bq, FOX_HD + LANES), BF16)],
        compiler_params=_params("parallel", "arbitrary"), name="fox_prompt")(q16, k16, kb16, vt16)


def _split_bf16(x, terms):
    parts = []
    for _ in range(terms):
        top = lax.bitcast_convert_type(lax.bitcast_convert_type(x, jnp.uint32) & jnp.uint32(0xFFFF0000), F32)
        parts.append(top.astype(BF16))
        x = x - top
    return jnp.stack(parts, axis=-1)


def _fox_sample_body(q_ref, kn_ref, vn_ref, kp_ref, vp_ref, c_ref, fo_any, o_ref, *, P, ST, HB):
    del fo_any
    r = lax.broadcasted_iota(jnp.int32, (ST, ST), 0)
    cc = lax.broadcasted_iota(jnp.int32, (ST, ST), 1)
    hd = FOX_HD
    kp = pltpu.einshape("phd->hpd", kp_ref[...]).astype(BF16)
    vp = pltpu.einshape("phd->hpd", vp_ref[...]).astype(BF16)
    scores = []
    for h in range(HB):
        sl = slice(h * hd, (h + 1) * hd)
        q = q_ref[:, sl]
        c = c_ref[h]
        cb = c[:, P - 1:P]
        sp = (lax.dot_general(q, kp[h], _NT, preferred_element_type=F32)
              + (cb - c[:, :P]) * LOG2E)
        sn = (lax.dot_general(q, kn_ref[:, sl], _NT, preferred_element_type=F32)
              + (cb - c[:, P:]) * LOG2E)
        scores.append((sp, jnp.where(cc <= r, sn, -jnp.inf)))
    for h, (sp, sn) in enumerate(scores):
        sl = slice(h * hd, (h + 1) * hd)
        m = jnp.maximum(jnp.max(sp, axis=1, keepdims=True), jnp.max(sn, axis=1, keepdims=True))
        pp = jnp.exp2(sp - m)
        pn = jnp.exp2(sn - m)
        l = jnp.sum(pp, axis=1, keepdims=True) + jnp.sum(pn, axis=1, keepdims=True)
        o = (jnp.dot(pp.astype(BF16), vp[h], preferred_element_type=F32)
             + jnp.dot(pn.astype(BF16), vn_ref[:, sl], preferred_element_type=F32))
        o_ref[:, sl] = (o / l).astype(o_ref.dtype)


def _fox_sample(fo, q16, k16, v16, cache_k, cache_v, ct, layer, T, SB, ST):
    M, W = q16.shape
    H = W // FOX_HD
    P = cache_k.shape[2]
    r0 = T // ST
    HB = SUBLANES
    assert H % HB == 0
    new = pl.BlockSpec((ST, HB * FOX_HD), lambda b, h: (r0 + b, h))
    past = pl.BlockSpec((None, None, P, HB, FOX_HD), lambda b, h: (layer, b, 0, h, 0))
    return pl.pallas_call(
        functools.partial(_fox_sample_body, P=P, ST=ST, HB=HB), grid=(SB, H // HB),
        in_specs=[new, new, new, past, past,
                  pl.BlockSpec((None, HB, 1, P + ST), lambda b, h: (b, h, 0, 0)),
                  pl.BlockSpec(memory_space=pl.ANY)],
        out_specs=new,
        out_shape=jax.ShapeDtypeStruct((M, W), BF16),
        input_output_aliases={6: 0},
        compiler_params=_params("parallel", "arbitrary"), name="fox_sample")(
            q16, k16, v16, cache_k, cache_v, ct, fo)


def _gla_body(q_ref, k_ref, v_ref, r_ref, la_ref, s0_ref, g_ref, go_any, o_ref, sout_ref,
              S_sc, b_sc, q_sc, *, C, NH, DK, DV, nc):
    del go_any
    c = pl.program_id(1)

    @pl.when(c == 0)
    def _():
        S_sc[...] = s0_ref[...]

    b = _row_cumsum(la_ref[...])
    b_sc[...] = b
    q = q_ref[...] * (DK ** -0.5)
    q_sc[...] = q
    bl = b[C - 1:C, :]
    qe = (q * jnp.exp(b)).astype(BF16)
    kdec = (k_ref[...] * jnp.exp(bl - b)).astype(BF16)
    ebl = jnp.exp(bl)
    v16 = v_ref[...].astype(BF16)

    R = GLA_SUB
    nb = C // R
    k = k_ref[...]
    srow_c = lax.broadcasted_iota(jnp.int32, b.shape, 0)
    at_off = [None] * NH
    for i in range(1, nb):
        b_ref_i = b[R * i - 1:R * i, :]
        in_blk = (srow_c >> GLA_SUB_LOG2) == i
        qt = (q * jnp.exp(jnp.where(in_blk, b - b_ref_i, -jnp.inf))).astype(BF16)
        kt = (k * jnp.exp(jnp.where(srow_c < R * i, b_ref_i - b, -jnp.inf))).astype(BF16)
        for h in range(NH):
            sl = slice(h * DK, (h + 1) * DK)
            part = lax.dot_general(kt[:, sl], qt[:, sl], _NT, preferred_element_type=F32)
            at_off[h] = part if at_off[h] is None else at_off[h] + part

    srow = lax.broadcasted_iota(jnp.int32, (R, DK), 0)
    lane = lax.broadcasted_iota(jnp.int32, (R, C), 1)
    at_diag = [[] for _ in range(NH)]
    for i in range(nb):
        blk = slice(R * i, R * (i + 1))
        ats = [jnp.zeros((R, C), F32) for _ in range(NH)]
        for r in range(R):
            t = R * i + r
            mask = srow <= r
            lm = lane == t
            for h in range(NH):
                sl = slice(h * DK, (h + 1) * DK)
                e = jnp.exp(jnp.where(mask, b_sc[t:t + 1, sl] - b_sc[blk, sl], -jnp.inf))
                w = e * (q_sc[t:t + 1, sl] * k_ref[blk, sl])
                ats[h] = jnp.where(lm, jnp.sum(w, axis=1, keepdims=True), ats[h])
        for h in range(NH):
            at_diag[h].append(ats[h])
    ats = []
    for h in range(NH):
        at = jnp.concatenate(at_diag[h], axis=0)
        ats.append(at if at_off[h] is None else at + at_off[h])

    g = g_ref[...]
    for h in range(NH):
        sl = slice(h * DK, (h + 1) * DK)
        vs = slice(h * DV, (h + 1) * DV)
        S = S_sc[h]
        o = (jnp.dot(qe[:, sl], S.astype(BF16), preferred_element_type=F32)
             + lax.dot_general(ats[h].astype(BF16), v16[:, vs], _TN, preferred_element_type=F32))
        r = r_ref[:, vs]
        o = _rms(o) * g * (r * (1.0 / (1.0 + jnp.exp(-r))))
        o_ref[:, vs] = o.astype(o_ref.dtype)
        dm = jnp.transpose(jnp.broadcast_to(ebl[:, sl], (DK, DK)))
        decay = jnp.concatenate([dm] * (DV // DK), axis=1)
        S_sc[h] = decay * S + lax.dot_general(kdec[:, sl], v16[:, vs], _TN, preferred_element_type=F32)

    @pl.when(c == nc - 1)
    def _():
        sout_ref[...] = S_sc[...]


def _gla(rest, la, s0, g, go, *, row0, nseq, nc, C, DK, DV):
    M = rest.shape[0]
    NH = GLA_HEADS
    KW, VW = NH * DK, NH * DV
    r0 = row0 // C
    assert row0 % C == 0 and VW == 2 * KW

    def rows(col):
        return lambda s, c: (r0 + s * nc + c, col)

    in_specs = [pl.BlockSpec((C, KW), rows(0)), pl.BlockSpec((C, KW), rows(1)),
                pl.BlockSpec((C, VW), rows(1)), pl.BlockSpec((C, VW), rows(2)),
                pl.BlockSpec((C, KW), rows(0)),
                pl.BlockSpec((None, NH, DK, DV), lambda s, c: (s, 0, 0, 0)),
                pl.BlockSpec((1, DV), lambda s, c: (0, 0))]
    args = [rest, rest, rest, rest, la, s0, g.reshape(1, DV)]
    aliases = {}
    if go is not None:
        in_specs.append(pl.BlockSpec(memory_space=pl.ANY))
        args.append(go)
        aliases = {7: 0}
        body = _gla_body
    else:
        body = functools.partial(_gla_body_noalias)
    return pl.pallas_call(
        functools.partial(body, C=C, NH=NH, DK=DK, DV=DV, nc=nc), grid=(nseq, nc),
        in_specs=in_specs,
        out_specs=[pl.BlockSpec((C, VW), rows(0)),
                   pl.BlockSpec((None, NH, DK, DV), lambda s, c: (s, 0, 0, 0))],
        out_shape=[jax.ShapeDtypeStruct((M, VW), BF16),
                   jax.ShapeDtypeStruct((nseq, NH, DK, DV), F32)],
        scratch_shapes=[pltpu.VMEM((NH, DK, DV), F32), pltpu.VMEM((C, KW), F32),
                        pltpu.VMEM((C, KW), F32)],
        input_output_aliases=aliases,
        compiler_params=_params("parallel", "arbitrary"), name="gla")(*args)


def _gla_body_noalias(q_ref, k_ref, v_ref, r_ref, la_ref, s0_ref, g_ref, o_ref, sout_ref,
                      S_sc, b_sc, q_sc, **kw):
    _gla_body(q_ref, k_ref, v_ref, r_ref, la_ref, s0_ref, g_ref, None, o_ref, sout_ref,
              S_sc, b_sc, q_sc, **kw)


def _pool_body(*refs, bm, pos0, GC, aliased):
    if aliased:
        p_ref, hist_ref, w_ref, sp_ref, _, o_ref, hout_ref, xa_sc = refs
    else:
        p_ref, hist_ref, w_ref, sp_ref, o_ref, hout_ref, xa_sc = refs
    i = pl.program_id(1)

    @pl.when(i == 0)
    def _():
        xa_sc[0:POOL_CARRY, :] = hist_ref[...]

    xa_sc[POOL_CARRY:POOL_CARRY + bm, :] = p_ref[...]
    pos = pos0 + i * bm + lax.broadcasted_iota(jnp.int32, (bm, 1), 0)
    for gi, w in enumerate(POOL_WINDOWS):
        cs = slice(gi * GC, (gi + 1) * GC)
        x = xa_sc[POOL_CARRY:POOL_CARRY + bm, cs]
        tot = x
        for d in range(1, w):
            tot = tot + xa_sc[POOL_CARRY - d:POOL_CARRY - d + bm, cs]
        cnt = jnp.minimum(pos + 1, w).astype(F32)
        dd = (tot / cnt - x).astype(BF16)
        y = jnp.dot(dd, w_ref[gi], preferred_element_type=F32) * sp_ref[:, cs]
        o_ref[:, cs] = y.astype(o_ref.dtype)
    last = xa_sc[bm:bm + POOL_CARRY, :]
    hout_ref[...] = last
    xa_sc[0:POOL_CARRY, :] = last


def _pool(rest, col, hist, w16, sp, po, *, row0, nseq, nb, bm, pos0):
    M = rest.shape[0]
    G, GC, _ = w16.shape
    PW = G * GC
    r0 = row0 // bm
    assert row0 % bm == 0 and bm >= POOL_CARRY
    in_specs = [pl.BlockSpec((bm, PW), lambda s, i: (r0 + s * nb + i, col)),
                pl.BlockSpec((None, POOL_CARRY, PW), lambda s, i: (s, 0, 0)),
                pl.BlockSpec((G, GC, GC), lambda s, i: (0, 0, 0)),
                pl.BlockSpec((1, PW), lambda s, i: (0, 0))]
    args = [rest, hist, w16, sp.reshape(1, PW)]
    aliases = {}
    if po is not None:
        in_specs.append(pl.BlockSpec(memory_space=pl.ANY))
        args.append(po)
        aliases = {4: 0}
    return pl.pallas_call(
        functools.partial(_pool_body, bm=bm, pos0=pos0, GC=GC, aliased=po is not None),
        grid=(nseq, nb), in_specs=in_specs,
        out_specs=[pl.BlockSpec((bm, PW), lambda s, i: (r0 + s * nb + i, 0)),
                   pl.BlockSpec((None, POOL_CARRY, PW), lambda s, i: (s, 0, 0))],
        out_shape=[jax.ShapeDtypeStruct((M, PW), BF16),
                   jax.ShapeDtypeStruct((nseq, POOL_CARRY, PW), F32)],
        scratch_shapes=[pltpu.VMEM((POOL_CARRY + bm, PW), F32)],
        input_output_aliases=aliases,
        compiler_params=_params("parallel", "arbitrary"), name="pool")(*args)


def kernel(x_prompt, x_sample, cache_fox_k, cache_fox_v, cache_fox_logf, state_gla, state_pool, g_pre_mix, w_in, b_fox_f, w_gla_a2, b_gla_a, g_gla_norm, w_pool, s_pool, w_out, g_post_mix, g_pre_mlp, w_up, w_down, g_post_mlp):
    B, T, D = x_prompt.shape
    SB, ST, _ = x_sample.shape
    L = w_in.shape[0]
    P = cache_fox_k.shape[2]
    assert B == 1
    FW = D // 2
    H = FW // FOX_HD
    GW = D // 4
    DV = GW // GLA_HEADS
    DK = DV // 2
    KW = GLA_HEADS * DK
    RANK = w_gla_a2.shape[1]
    PW = D - FW - GW
    GC = PW // len(POOL_WINDOWS)
    MS = SB * ST
    M = T + MS
    assert H + RANK <= LANES and 2 * KW + 2 * GW == 3 * PW and PW % LANES == 0

    o_fq, o_fk, o_fv, o_ff = 0, FW, 2 * FW, 3 * FW
    o_gq = o_ff + H
    o_ga = o_gq + 2 * KW + GW
    o_gr = o_ga + RANK

    w_in16, w_out16, w_up16 = (_cast_layer(w, 0) for w in (w_in, w_out, w_up))
    w_pool16 = w_pool.astype(BF16)
    past_logf = jnp.pad(cache_fox_logf.astype(F32), ((0, 0), (0, 0), (0, 0), (0, LANES - H)))
    zero_state = jnp.zeros((1, GLA_HEADS, DK, DV), F32)
    zero_hist = jnp.zeros((1, POOL_CARRY, PW), F32)
    sample_hist = jnp.pad(state_pool, ((0, 0), (0, 0), (POOL_CARRY - POOL_HIST, 0), (0, 0)))

    h = None
    x_groups = ((x_prompt.reshape(T, D), 0), (x_sample.reshape(MS, D), T))
    a = None
    for xg, row0 in x_groups:
        a = _norm_first(xg, g_pre_mix[0], a, row0, M)
    outs_p = [[] for _ in range(5)]
    outs_s = [[] for _ in range(5)]
    kv_p, kv_s = [None, None], [None, None]
    for l in range(L):
        bfox_pad = jnp.pad(b_fox_f[l].astype(F32), (0, LANES - H)).reshape(1, LANES)
        w2_pad = jnp.pad(w_gla_a2[l].astype(F32), ((H, LANES - H - RANK), (0, 0)))
        w_rest16 = jnp.concatenate([w_in16[:, :, o_gq:o_ga], w_in16[:, :, o_gr:]], axis=2)
        w_tail16 = jnp.concatenate([w_in16[:, :, o_ff:o_gq], w_in16[:, :, o_ga:o_gr],
                                    jnp.zeros((1, D, LANES - H - RANK), BF16)], axis=2)

        (q16,) = _matmul(a, w_in16, 0, [BF16], name="proj_q", col0=o_fq, ncols=FW,
                         scale=FOX_HD ** -0.5 * LOG2E)
        kv16 = []
        for t, (name, col0) in enumerate((("proj_k", o_fk), ("proj_v", o_fv))):
            common = dict(name=name, col0=col0, ncols=FW, layer=l, nlayers=L, hd=FOX_HD)
            kv_p[t], x16 = _matmul_heads(a, w_in16, kv_p[t], None, row0=0, nrows=T, **common)
            kv_s[t], x16 = _matmul_heads(a, w_in16, kv_s[t], x16, row0=T, nrows=MS, **common)
            kv16.append(x16)
        k16, v16 = kv16
        (rest,) = _matmul(a, w_rest16, 0, [F32], name="proj_rest")
        (tail,) = _matmul(a, w_tail16, 0, [F32], name="proj_tail")
        logf, la = _gates(tail, bfox_pad, w2_pad, b_gla_a[l].reshape(1, KW))

        c_p = _cumsum(logf[:T].reshape(1, T, LANES), 1024)
        bias = _split_bf16(c_p[0, :, :H] * (-LOG2E), BIAS_TERMS)
        kb16 = jnp.pad(bias.reshape(T, H * BIAS_TERMS), ((0, 0), (0, LANES - H * BIAS_TERMS)))
        vt16 = jnp.transpose(v16[:T])
        fo = _fox_prompt(q16, k16, kb16, vt16, T)
        lf_all = jnp.concatenate([past_logf[l], logf[T:].reshape(SB, ST, LANES)], axis=1)
        c_s = _cumsum(lf_all, P + ST)
        ct_s = jnp.transpose(c_s[:, :, :H], (0, 2, 1)).reshape(SB, H, 1, P + ST)
        fo = _fox_sample(fo, q16, k16, v16, cache_fox_k, cache_fox_v, ct_s, l, T, SB, ST)

        go, gla_p = _gla(rest, la, zero_state, g_gla_norm[l], None,
                         row0=0, nseq=1, nc=T // GLA_CHUNK, C=GLA_CHUNK, DK=DK, DV=DV)
        go, gla_s = _gla(rest, la, state_gla[l], g_gla_norm[l], go,
                         row0=T, nseq=SB, nc=1, C=ST, DK=DK, DV=DV)

        pcol = (2 * KW + 2 * GW) // PW
        bp = _pick(T, 512)
        po, hist_p = _pool(rest, pcol, zero_hist, w_pool16[l], s_pool[l], None,
                           row0=0, nseq=1, nb=T // bp, bm=bp, pos0=0)
        po, hist_s = _pool(rest, pcol, sample_hist[l], w_pool16[l], s_pool[l], po,
                           row0=T, nseq=SB, nb=1, bm=ST, pos0=P)

        y = _matmul_cat([fo, go, po], w_out16, 0, name="proj_out")
        if h is None:
            hc = None
            for xg, row0 in x_groups:
                hc = _norm_mid(xg, y, g_post_mix[l], g_pre_mlp[l], hc, row0)
            h, c = hc
        else:
            h, c = _norm_mid(h, y, g_post_mix[l], g_pre_mlp[l])
        later = [(w, l + 1) for w in (w_up, w_in, w_out)] if l + 1 < L else []
        hid, w_down16, *nxt = _matmul(c, w_up16, 0, [BF16], name="mlp_up", relu2=True,
                                      side=[(w_down, l)] + later)
        f = _matmul_kacc(hid, w_down16, 0, name="mlp_down")
        if nxt:
            w_up16, w_in16, w_out16 = nxt
        if l + 1 < L:
            h, a = _norm_mid(h, f, g_post_mlp[l], g_pre_mix[l + 1])
        else:
            y_prompt = _norm_last(h, f, g_post_mlp[l], 0, T)
            y_sample = _norm_last(h, f, g_post_mlp[l], T, MS)

        outs_p[2].append(logf[:T, :H].reshape(1, T, H))
        outs_p[3].append(gla_p)
        outs_p[4].append(hist_p[:, POOL_CARRY - POOL_HIST:])
        outs_s[2].append(logf[T:, :H].reshape(SB, ST, H))
        outs_s[3].append(gla_s)
        outs_s[4].append(hist_s[:, POOL_CARRY - POOL_HIST:])

    p_kv = [x.reshape(L, 1, T, H, FOX_HD) for x in kv_p]
    s_kv = [x.reshape(L, SB, ST, H, FOX_HD) for x in kv_s]
    return (y_prompt.reshape(1, T, D), y_sample.reshape(SB, ST, D),
            *p_kv, *[jnp.stack(o) for o in outs_p[2:]], *s_kv, *[jnp.stack(o) for o in outs_s[2:]])
```

```python
import functools
import math

import jax
import jax.numpy as jnp
from jax import lax
from jax.experimental import pallas as pl
from jax.experimental.pallas import tpu as pltpu

F32 = jnp.float32
BF16 = jnp.bfloat16

RMS_EPS = 1e-6
FOX_HD = 128
GLA_HEADS = 4
GLA_TAU = 16.0
GLA_CHUNK = 64
GLA_SUB_LOG2 = 4
GLA_SUB = 1 << GLA_SUB_LOG2
POOL_WINDOWS = (2, 4, 8, 16)
POOL_HIST = 15
POOL_CARRY = 16
LANES = 128
SUBLANES = 8
BF16_ROWS = 16
ONES_ROWS = BF16_ROWS
LOG2E = 1.4426950408889634
BIAS_TERMS = 3
VMEM_LIMIT = 56 * 1024 * 1024
CAST_BLOCK_BYTES = 4 * 1024 * 1024


def _params(*sem):
    return pltpu.CompilerParams(dimension_semantics=sem, vmem_limit_bytes=VMEM_LIMIT)


def _pick(n, pref, mult=8):
    b = min(pref, n)
    b -= b % mult
    while b > 0 and n % b:
        b -= mult
    assert b > 0, (n, pref, mult)
    return b


def _rms(x):
    return x * lax.rsqrt(jnp.mean(x * x, axis=-1, keepdims=True) + RMS_EPS)


def _log_sigmoid(x):
    return jnp.minimum(x, 0.0) - jnp.log1p(jnp.exp(-jnp.abs(x)))


def _norm_first_body(h_ref, g_ref, a_ref):
    a_ref[...] = (_rms(h_ref[...]) * g_ref[...]).astype(a_ref.dtype)


def _norm_mid_body(h_ref, y_ref, gy_ref, g_ref, h_out, a_out):
    h = h_ref[...] + _rms(y_ref[...]) * gy_ref[...]
    h_out[...] = h
    a_out[...] = (_rms(h) * g_ref[...]).astype(a_out.dtype)


def _norm_last_body(h_ref, y_ref, gy_ref, h_out):
    h_out[...] = h_ref[...] + _rms(y_ref[...]) * gy_ref[...]


def _row_block(row0, nrows, pref):
    return _pick(math.gcd(row0, nrows) if row0 else nrows, pref)


def _norm_first(x, g, prev, row0, M):
    nrows, D = x.shape
    bm = _row_block(row0, nrows, 256)
    r0 = row0 // bm
    in_specs = [pl.BlockSpec((bm, D), lambda i: (i, 0)), pl.BlockSpec((1, D), lambda i: (0, 0))]
    args = [x, g.reshape(1, D)]
    if prev is not None:
        in_specs.append(pl.BlockSpec(memory_space=pl.ANY))
        args.append(prev)
    return pl.pallas_call(
        lambda x_ref, g_ref, *rest: _norm_first_body(x_ref, g_ref, rest[-1]),
        grid=(nrows // bm,), in_specs=in_specs, out_specs=pl.BlockSpec((bm, D), lambda i: (r0 + i, 0)),
        out_shape=jax.ShapeDtypeStruct((M, D), BF16),
        input_output_aliases={2: 0} if prev is not None else {},
        compiler_params=_params("parallel"), name="norm_first")(*args)


def _norm_mid(h_src, y, gy, g, prev=None, row0=0):
    nrows, D = h_src.shape
    M = y.shape[0]
    bm = _row_block(row0, nrows, 256)
    r0 = row0 // bm
    src = pl.BlockSpec((bm, D), lambda i: (i, 0))
    row = pl.BlockSpec((bm, D), lambda i: (r0 + i, 0))
    vec = pl.BlockSpec((1, D), lambda i: (0, 0))
    in_specs = [src, row, vec, vec]
    args = [h_src, y, gy.reshape(1, D), g.reshape(1, D)]
    if prev is not None:
        in_specs += [pl.BlockSpec(memory_space=pl.ANY)] * 2
        args += list(prev)
    return pl.pallas_call(
        lambda h_ref, y_ref, gy_ref, g_ref, *rest: _norm_mid_body(h_ref, y_ref, gy_ref, g_ref, *rest[-2:]),
        grid=(nrows // bm,), in_specs=in_specs, out_specs=[row, row],
        out_shape=[jax.ShapeDtypeStruct((M, D), F32), jax.ShapeDtypeStruct((M, D), BF16)],
        input_output_aliases={4: 0, 5: 1} if prev is not None else {},
        compiler_params=_params("parallel"), name="norm_mid")(*args)


def _norm_last(h, y, gy, row0, nrows):
    D = h.shape[1]
    bm = _pick(math.gcd(row0, nrows) if row0 else nrows, 256)
    r0 = row0 // bm
    row = pl.BlockSpec((bm, D), lambda i: (r0 + i, 0))
    vec = pl.BlockSpec((1, D), lambda i: (0, 0))
    return pl.pallas_call(
        _norm_last_body, grid=(nrows // bm,), in_specs=[row, row, vec],
        out_specs=pl.BlockSpec((bm, D), lambda i: (i, 0)),
        out_shape=jax.ShapeDtypeStruct((nrows, D), F32),
        compiler_params=_params("parallel"), name="norm_last")(h, y, gy.reshape(1, D))


def _mm_body(x_ref, w_ref, *refs, relu2, scale, nside):
    side_in, o_refs, side_out = refs[:nside], refs[nside:len(refs) - nside], refs[len(refs) - nside:]
    acc = jnp.dot(x_ref[...], w_ref[...], preferred_element_type=F32)
    if relu2:
        acc = jnp.square(jnp.maximum(acc, 0.0))
    if scale is not None:
        acc = acc * scale
    for o in o_refs:
        o[...] = acc.astype(o.dtype)
    for s_in, s_out in zip(side_in, side_out):
        s_out[...] = s_in[...].astype(s_out.dtype)


def _matmul(x, w, layer, out_dtypes, *, name, relu2=False, scale=None, col0=0, ncols=None, bm=1024, bn=1024,
            side=()):
    M, K = x.shape
    N = w.shape[2] - col0 if ncols is None else ncols
    bm = _pick(M, bm)
    bn = _pick(N, bn, LANES)
    assert col0 % bn == 0
    c0 = col0 // bn
    nj = N // bn
    steps = (M // bm) * nj
    in_specs = [pl.BlockSpec((bm, K), lambda i, j: (i, 0)),
                pl.BlockSpec((None, K, bn), lambda i, j: (layer, 0, c0 + j))]
    out_specs = [pl.BlockSpec((bm, bn), lambda i, j: (i, j)) for _ in out_dtypes]
    out_shape = [jax.ShapeDtypeStruct((M, N), dt) for dt in out_dtypes]
    args = [x, w]
    for src, src_layer in side:
        _, R, C = src.shape
        rb = next(r for r in range(BF16_ROWS, R + 1, BF16_ROWS) if R % r == 0 and R // r <= steps)
        last = R // rb - 1
        in_specs.append(pl.BlockSpec(
            (None, rb, C), lambda i, j, sl=src_layer, last=last: (sl, jnp.minimum(i * nj + j, last), 0)))
        out_specs.append(pl.BlockSpec(
            (None, rb, C), lambda i, j, last=last: (0, jnp.minimum(i * nj + j, last), 0)))
        out_shape.append(jax.ShapeDtypeStruct((1, R, C), BF16))
        args.append(src)
    sem = ("arbitrary", "arbitrary") if side else ("parallel", "arbitrary")
    return pl.pallas_call(
        functools.partial(_mm_body, relu2=relu2, scale=scale, nside=len(side)), grid=(M // bm, nj),
        in_specs=in_specs, out_specs=out_specs, out_shape=out_shape,
        compiler_params=_params(*sem), name=name)(*args)


def _cast_body(w_ref, o_ref):
    o_ref[...] = w_ref[...].astype(o_ref.dtype)


def _cast_layer(w, layer):
    _, R, C = w.shape
    rb = _pick(R, max(BF16_ROWS, CAST_BLOCK_BYTES // (4 * C)), BF16_ROWS)
    return pl.pallas_call(
        _cast_body, grid=(R // rb,),
        in_specs=[pl.BlockSpec((None, rb, C), lambda i: (layer, i, 0))],
        out_specs=pl.BlockSpec((None, rb, C), lambda i: (0, i, 0)),
        out_shape=jax.ShapeDtypeStruct((1, R, C), BF16),
        compiler_params=_params("parallel"), name="cast_weight")(w)


def _mm_heads_body(x_ref, w_ref, *refs, hd):
    o32_ref, o16_ref = refs[-2:]
    acc = jnp.dot(x_ref[...], w_ref[...], preferred_element_type=F32)
    o16_ref[...] = acc.astype(o16_ref.dtype)
    o32_ref[...] = pltpu.einshape("m(hd)->mhd", acc, d=hd)


def _matmul_heads(x, w, prev32, prev16, *, name, col0, ncols, row0, nrows, layer, nlayers, hd, bm=512, bn=1024):
    M, K = x.shape
    bm = _pick(math.gcd(row0, nrows) if row0 else nrows, bm)
    bn = _pick(ncols, bn, LANES * SUBLANES)
    assert col0 % bn == 0 and bn % hd == 0
    c0, r0, hb = col0 // bn, row0 // bm, bn // hd
    in_specs = [pl.BlockSpec((bm, K), lambda i, j: (r0 + i, 0)),
                pl.BlockSpec((None, K, bn), lambda i, j: (layer, 0, c0 + j))]
    args = [x, w]
    aliases = {}
    for out_idx, prev in enumerate((prev32, prev16)):
        if prev is not None:
            aliases[len(args)] = out_idx
            in_specs.append(pl.BlockSpec(memory_space=pl.ANY))
            args.append(prev)
    return pl.pallas_call(
        functools.partial(_mm_heads_body, hd=hd), grid=(nrows // bm, ncols // bn), in_specs=in_specs,
        out_specs=[pl.BlockSpec((None, bm, hb, hd), lambda i, j: (layer, i, j, 0)),
                   pl.BlockSpec((bm, bn), lambda i, j: (r0 + i, j))],
        out_shape=[jax.ShapeDtypeStruct((nlayers, nrows, ncols // hd, hd), F32),
                   jax.ShapeDtypeStruct((M, ncols), BF16)],
        input_output_aliases=aliases,
        compiler_params=_params("parallel", "arbitrary"), name=name)(*args)


def _mm_cat_body(*refs, splits):
    x_refs, w_ref, o_ref = refs[:-2], refs[-2], refs[-1]
    acc = None
    off = 0
    for x_ref, kk in zip(x_refs, splits):
        part = jnp.dot(x_ref[...], w_ref[off:off + kk, :], preferred_element_type=F32)
        acc = part if acc is None else acc + part
        off += kk
    o_ref[...] = acc


def _matmul_cat(xs, w, layer, *, name, bm=1024, bn=1024):
    M = xs[0].shape[0]
    _, K, N = w.shape
    splits = tuple(x.shape[1] for x in xs)
    assert sum(splits) == K
    bm = _pick(M, bm)
    bn = _pick(N, bn, LANES)
    in_specs = [pl.BlockSpec((bm, kk), lambda i, j: (i, 0)) for kk in splits]
    in_specs.append(pl.BlockSpec((None, K, bn), lambda i, j: (layer, 0, j)))
    return pl.pallas_call(
        functools.partial(_mm_cat_body, splits=splits), grid=(M // bm, N // bn),
        in_specs=in_specs, out_specs=pl.BlockSpec((bm, bn), lambda i, j: (i, j)),
        out_shape=jax.ShapeDtypeStruct((M, N), F32),
        compiler_params=_params("parallel", "arbitrary"), name=name)(*xs, w)


def _mm_kacc_body(x_ref, w_ref, o_ref, *, bn):
    k = pl.program_id(1)

    @pl.when(k == 0)
    def _():
        o_ref[...] = jnp.zeros(o_ref.shape, F32)

    x = x_ref[...]
    for n0 in range(0, o_ref.shape[1], bn):
        o_ref[:, n0:n0 + bn] += jnp.dot(x, w_ref[:, n0:n0 + bn], preferred_element_type=F32)


def _matmul_kacc(x, w, layer, *, name, bm=512, bk=2048):
    M, K = x.shape
    N = w.shape[2]
    bm = _pick(M, bm)
    bk = _pick(K, bk, LANES)
    return pl.pallas_call(
        functools.partial(_mm_kacc_body, bn=_pick(N, 512, LANES)), grid=(M // bm, K // bk),
        in_specs=[pl.BlockSpec((bm, bk), lambda i, k: (i, k)),
                  pl.BlockSpec((None, bk, N), lambda i, k: (layer, k, 0))],
        out_specs=pl.BlockSpec((bm, N), lambda i, k: (i, 0)),
        out_shape=jax.ShapeDtypeStruct((M, N), F32),
        compiler_params=_params("parallel", "arbitrary"), name=name)(x, w)


def _gates_body(t_ref, bf_ref, w2_ref, bg_ref, logf_ref, la_ref):
    t = t_ref[...]
    logf_ref[...] = _log_sigmoid(t + bf_ref[...])
    zg = jnp.dot(t, w2_ref[...], preferred_element_type=F32) + bg_ref[...]
    la_ref[...] = _log_sigmoid(zg) * (1.0 / GLA_TAU)


def _gates(tail, bfox_pad, w2_pad, bgla):
    M = tail.shape[0]
    KW = w2_pad.shape[1]
    bm = _pick(M, 512)
    return pl.pallas_call(
        _gates_body, grid=(M // bm,),
        in_specs=[pl.BlockSpec((bm, LANES), lambda i: (i, 0)),
                  pl.BlockSpec((1, LANES), lambda i: (0, 0)),
                  pl.BlockSpec((LANES, KW), lambda i: (0, 0)),
                  pl.BlockSpec((1, KW), lambda i: (0, 0))],
        out_specs=[pl.BlockSpec((bm, LANES), lambda i: (i, 0)),
                   pl.BlockSpec((bm, KW), lambda i: (i, 0))],
        out_shape=[jax.ShapeDtypeStruct((M, LANES), F32), jax.ShapeDtypeStruct((M, KW), F32)],
        compiler_params=_params("parallel"), name="gates")(tail, bfox_pad, w2_pad, bgla)


def _row_cumsum(x):
    n = x.shape[0]
    row = lax.broadcasted_iota(jnp.int32, x.shape, 0)
    sh = 1
    while sh < n:
        x = x + jnp.where(row >= sh, pltpu.roll(x, sh, axis=0), 0.0)
        sh *= 2
    return x


def _cumsum_body(x_ref, c_ref, carry_sc):
    @pl.when(pl.program_id(1) == 0)
    def _():
        carry_sc[...] = jnp.zeros(carry_sc.shape, F32)

    c = _row_cumsum(x_ref[...]) + carry_sc[...]
    c_ref[...] = c
    n = c.shape[0]
    carry_sc[...] = c[n - 1:n, :]


def _cumsum(x, bl):
    S, L, W = x.shape
    bl = _pick(L, bl)
    spec = pl.BlockSpec((None, bl, W), lambda s, i: (s, i, 0))
    return pl.pallas_call(
        _cumsum_body, grid=(S, L // bl), in_specs=[spec], out_specs=spec,
        out_shape=jax.ShapeDtypeStruct((S, L, W), F32),
        scratch_shapes=[pltpu.VMEM((1, W), F32)],
        compiler_params=_params("parallel", "arbitrary"), name="cumsum")(x)


_NT = (((1,), (1,)), ((), ()))
_TN = (((0,), (0,)), ((), ()))


def _fox_prompt_body(q_ref, k_ref, kb_ref, vt_ref, o_ref, acc_sc, st_sc, qa_sc, *, tq, tk, nsub):
    qi = pl.program_id(1)
    hd = vt_ref.shape[0]
    bq = nsub * tq
    nkb = bq // tk
    ones = jnp.ones((ONES_ROWS, tk), BF16)
    lane = lax.broadcasted_iota(jnp.int32, (bq, LANES), 1) - pl.program_id(0) * BIAS_TERMS
    qa_sc[:, :hd] = q_ref[...]
    qa_sc[:, hd:] = jnp.where(lane.astype(jnp.uint32) < BIAS_TERMS, 1.0, 0.0).astype(BF16)

    def mask(st, mask_off):
        key = lax.broadcasted_iota(jnp.int32, (tk, tq), 0) + mask_off
        qry = lax.broadcasted_iota(jnp.int32, (tk, tq), 1)
        return jnp.where(key <= qry, st, -jnp.inf)

    def scores(j, sub):
        s0 = pl.multiple_of(j * tk, tk)
        k = jnp.concatenate([k_ref[pl.ds(s0, tk), :], kb_ref[pl.ds(s0, tk), :]], axis=1)
        return lax.dot_general(k, qa_sc[sub * tq:(sub + 1) * tq, :], _NT,
                               preferred_element_type=F32)

    def accumulate(j, sub, st, m_blk, m_prev):
        s0 = pl.multiple_of(j * tk, tk)
        m_new = jnp.maximum(m_prev, m_blk)
        alpha = jnp.exp2(m_prev - m_new)
        pt = jnp.exp2(st - m_new).astype(BF16)
        vt = jnp.concatenate([vt_ref[:, pl.ds(s0, tk)], ones], axis=0)
        acc_sc[sub] = alpha * acc_sc[sub] + jnp.dot(vt, pt, preferred_element_type=F32)
        return m_new

    def issue(j, slot):
        mb = []
        for sub in range(nsub):
            st = scores(j, sub)
            st_sc[slot, sub] = st
            mb.append(jnp.max(st, axis=0, keepdims=True))
        return tuple(mb)

    acc_sc[...] = jnp.zeros(acc_sc.shape, F32)
    m0 = tuple(jnp.full((1, tq), -jnp.inf, F32) for _ in range(nsub))
    n = nkb * qi

    def body(jj, carry):
        ms, mb = carry
        for u in range(nkb):
            slot = u % 2
            j = nkb * jj + u
            mb_next = issue(j + 1, 1 - slot)
            ms = tuple(accumulate(j, sub, st_sc[slot, sub], mb[sub], ms[sub]) for sub in range(nsub))
            mb = mb_next
        return ms, mb

    assert nkb % 2 == 0
    ms, _ = lax.fori_loop(0, qi, body, (m0, issue(0, 0)))
    ms = list(ms)
    band = [(d, sub) for d in range(nkb) for sub in range(nsub) if d * tk <= (sub + 1) * tq - 1]
    late = {(d, sub): scores(n + d, sub) for d, sub in band if d > 0}
    for d, sub in band:
        k_lo, k_hi, q_lo = d * tk, (d + 1) * tk - 1, sub * tq
        st = st_sc[0, sub] if d == 0 else late[d, sub]
        if k_hi > q_lo:
            st = mask(st, k_lo - q_lo)
        ms[sub] = accumulate(n + d, sub, st, jnp.max(st, axis=0, keepdims=True), ms[sub])
    for sub in range(nsub):
        acc = acc_sc[sub]
        o = acc[:hd, :] / acc[hd:hd + 1, :]
        o_ref[sub * tq:(sub + 1) * tq, :] = jnp.transpose(o).astype(o_ref.dtype)


FOX_TQ, FOX_TK, FOX_NSUB = 512, 512, 4


def _fox_prompt(q16, k16, kb16, vt16, T):
    M, W = q16.shape
    H = W // FOX_HD
    tq, tk, nsub = FOX_TQ, FOX_TK, FOX_NSUB
    bq = nsub * tq
    assert T % bq == 0 and bq % tk == 0 and FOX_HD == LANES and H * BIAS_TERMS <= LANES
    return pl.pallas_call(
        functools.partial(_fox_prompt_body, tq=tq, tk=tk, nsub=nsub), grid=(H, T // bq),
        in_specs=[pl.BlockSpec((bq, FOX_HD), lambda h, i: (i, h)),
                  pl.BlockSpec((T, FOX_HD), lambda h, i: (0, h)),
                  pl.BlockSpec((T, LANES), lambda h, i: (0, 0)),
                  pl.BlockSpec((FOX_HD, T), lambda h, i: (h, 0))],
        out_specs=pl.BlockSpec((bq, FOX_HD), lambda h, i: (i, h)),
        out_shape=jax.ShapeDtypeStruct((M, W), BF16),
        scratch_shapes=[pltpu.VMEM((nsub, FOX_HD + ONES_ROWS, tq), F32),
                        pltpu.VMEM((2, nsub, tk, tq), F32),
                        pltpu.VMEM((bq, FOX_HD + LANES), BF16)],
        compiler_params=_params("parallel", "arbitrary"), name="fox_prompt")(q16, k16, kb16, vt16)


def _split_bf16(x, terms):
    parts = []
    for _ in range(terms):
        top = lax.bitcast_convert_type(lax.bitcast_convert_type(x, jnp.uint32) & jnp.uint32(0xFFFF0000), F32)
        parts.append(top.astype(BF16))
        x = x - top
    return jnp.stack(parts, axis=-1)


def _fox_sample_body(q_ref, kn_ref, vn_ref, kp_ref, vp_ref, c_ref, fo_any, o_ref, *, P, ST, HB):
    del fo_any
    r = lax.broadcasted_iota(jnp.int32, (ST, ST), 0)
    cc = lax.broadcasted_iota(jnp.int32, (ST, ST), 1)
    hd = FOX_HD
    kp = pltpu.einshape("phd->hpd", kp_ref[...]).astype(BF16)
    vp = pltpu.einshape("phd->hpd", vp_ref[...]).astype(BF16)
    scores = []
    for h in range(HB):
        sl = slice(h * hd, (h + 1) * hd)
        q = q_ref[:, sl]
        c = c_ref[h]
        cb = c[:, P - 1:P]
        sp = (lax.dot_general(q, kp[h], _NT, preferred_element_type=F32)
              + (cb - c[:, :P]) * LOG2E)
        sn = (lax.dot_general(q, kn_ref[:, sl], _NT, preferred_element_type=F32)
              + (cb - c[:, P:]) * LOG2E)
        scores.append((sp, jnp.where(cc <= r, sn, -jnp.inf)))
    for h, (sp, sn) in enumerate(scores):
        sl = slice(h * hd, (h + 1) * hd)
        m = jnp.maximum(jnp.max(sp, axis=1, keepdims=True), jnp.max(sn, axis=1, keepdims=True))
        pp = jnp.exp2(sp - m)
        pn = jnp.exp2(sn - m)
        l = jnp.sum(pp, axis=1, keepdims=True) + jnp.sum(pn, axis=1, keepdims=True)
        o = (jnp.dot(pp.astype(BF16), vp[h], preferred_element_type=F32)
             + jnp.dot(pn.astype(BF16), vn_ref[:, sl], preferred_element_type=F32))
        o_ref[:, sl] = (o / l).astype(o_ref.dtype)


def _fox_sample(fo, q16, k16, v16, cache_k, cache_v, ct, layer, T, SB, ST):
    M, W = q16.shape
    H = W // FOX_HD
    P = cache_k.shape[2]
    r0 = T // ST
    HB = SUBLANES
    assert H % HB == 0
    new = pl.BlockSpec((ST, HB * FOX_HD), lambda b, h: (r0 + b, h))
    past = pl.BlockSpec((None, None, P, HB, FOX_HD), lambda b, h: (layer, b, 0, h, 0))
    return pl.pallas_call(
        functools.partial(_fox_sample_body, P=P, ST=ST, HB=HB), grid=(SB, H // HB),
        in_specs=[new, new, new, past, past,
                  pl.BlockSpec((None, HB, 1, P + ST), lambda b, h: (b, h, 0, 0)),
                  pl.BlockSpec(memory_space=pl.ANY)],
        out_specs=new,
        out_shape=jax.ShapeDtypeStruct((M, W), BF16),
        input_output_aliases={6: 0},
        compiler_params=_params("parallel", "arbitrary"), name="fox_sample")(
            q16, k16, v16, cache_k, cache_v, ct, fo)


def _gla_body(q_ref, k_ref, v_ref, r_ref, la_ref, s0_ref, g_ref, go_any, o_ref, sout_ref,
              S_sc, b_sc, q_sc, *, C, NH, DK, DV, nc):
    del go_any
    c = pl.program_id(1)

    @pl.when(c == 0)
    def _():
        S_sc[...] = s0_ref[...]

    b = _row_cumsum(la_ref[...])
    b_sc[...] = b
    q = q_ref[...] * (DK ** -0.5)
    q_sc[...] = q
    bl = b[C - 1:C, :]
    qe = (q * jnp.exp(b)).astype(BF16)
    kdec = (k_ref[...] * jnp.exp(bl - b)).astype(BF16)
    ebl = jnp.exp(bl)
    v16 = v_ref[...].astype(BF16)

    R = GLA_SUB
    nb = C // R
    k = k_ref[...]
    srow_c = lax.broadcasted_iota(jnp.int32, b.shape, 0)
    at_off = [None] * NH
    for i in range(1, nb):
        b_ref_i = b[R * i - 1:R * i, :]
        in_blk = (srow_c >> GLA_SUB_LOG2) == i
        qt = (q * jnp.exp(jnp.where(in_blk, b - b_ref_i, -jnp.inf))).astype(BF16)
        kt = (k * jnp.exp(jnp.where(srow_c < R * i, b_ref_i - b, -jnp.inf))).astype(BF16)
        for h in range(NH):
            sl = slice(h * DK, (h + 1) * DK)
            part = lax.dot_general(kt[:, sl], qt[:, sl], _NT, preferred_element_type=F32)
            at_off[h] = part if at_off[h] is None else at_off[h] + part

    srow = lax.broadcasted_iota(jnp.int32, (R, DK), 0)
    lane = lax.broadcasted_iota(jnp.int32, (R, C), 1)
    at_diag = [[] for _ in range(NH)]
    for i in range(nb):
        blk = slice(R * i, R * (i + 1))
        ats = [jnp.zeros((R, C), F32) for _ in range(NH)]
        for r in range(R):
            t = R * i + r
            mask = srow <= r
            lm = lane == t
            for h in range(NH):
                sl = slice(h * DK, (h + 1) * DK)
                e = jnp.exp(jnp.where(mask, b_sc[t:t + 1, sl] - b_sc[blk, sl], -jnp.inf))
                w = e * (q_sc[t:t + 1, sl] * k_ref[blk, sl])
                ats[h] = jnp.where(lm, jnp.sum(w, axis=1, keepdims=True), ats[h])
        for h in range(NH):
            at_diag[h].append(ats[h])
    ats = []
    for h in range(NH):
        at = jnp.concatenate(at_diag[h], axis=0)
        ats.append(at if at_off[h] is None else at + at_off[h])

    g = g_ref[...]
    for h in range(NH):
        sl = slice(h * DK, (h + 1) * DK)
        vs = slice(h * DV, (h + 1) * DV)
        S = S_sc[h]
        o = (jnp.dot(qe[:, sl], S.astype(BF16), preferred_element_type=F32)
             + lax.dot_general(ats[h].astype(BF16), v16[:, vs], _TN, preferred_element_type=F32))
        r = r_ref[:, vs]
        o = _rms(o) * g * (r * (1.0 / (1.0 + jnp.exp(-r))))
        o_ref[:, vs] = o.astype(o_ref.dtype)
        dm = jnp.transpose(jnp.broadcast_to(ebl[:, sl], (DK, DK)))
        decay = jnp.concatenate([dm] * (DV // DK), axis=1)
        S_sc[h] = decay * S + lax.dot_general(kdec[:, sl], v16[:, vs], _TN, preferred_element_type=F32)

    @pl.when(c == nc - 1)
    def _():
        sout_ref[...] = S_sc[...]


def _gla(rest, la, s0, g, go, *, row0, nseq, nc, C, DK, DV):
    M = rest.shape[0]
    NH = GLA_HEADS
    KW, VW = NH * DK, NH * DV
    r0 = row0 // C
    assert row0 % C == 0 and VW == 2 * KW

    def rows(col):
        return lambda s, c: (r0 + s * nc + c, col)

    in_specs = [pl.BlockSpec((C, KW), rows(0)), pl.BlockSpec((C, KW), rows(1)),
                pl.BlockSpec((C, VW), rows(1)), pl.BlockSpec((C, VW), rows(2)),
                pl.BlockSpec((C, KW), rows(0)),
                pl.BlockSpec((None, NH, DK, DV), lambda s, c: (s, 0, 0, 0)),
                pl.BlockSpec((1, DV), lambda s, c: (0, 0))]
    args = [rest, rest, rest, rest, la, s0, g.reshape(1, DV)]
    aliases = {}
    if go is not None:
        in_specs.append(pl.BlockSpec(memory_space=pl.ANY))
        args.append(go)
        aliases = {7: 0}
        body = _gla_body
    else:
        body = functools.partial(_gla_body_noalias)
    return pl.pallas_call(
        functools.partial(body, C=C, NH=NH, DK=DK, DV=DV, nc=nc), grid=(nseq, nc),
        in_specs=in_specs,
        out_specs=[pl.BlockSpec((C, VW), rows(0)),
                   pl.BlockSpec((None, NH, DK, DV), lambda s, c: (s, 0, 0, 0))],
        out_shape=[jax.ShapeDtypeStruct((M, VW), BF16),
                   jax.ShapeDtypeStruct((nseq, NH, DK, DV), F32)],
        scratch_shapes=[pltpu.VMEM((NH, DK, DV), F32), pltpu.VMEM((C, KW), F32),
                        pltpu.VMEM((C, KW), F32)],
        input_output_aliases=aliases,
        compiler_params=_params("parallel", "arbitrary"), name="gla")(*args)


def _gla_body_noalias(q_ref, k_ref, v_ref, r_ref, la_ref, s0_ref, g_ref, o_ref, sout_ref,
                      S_sc, b_sc, q_sc, **kw):
    _gla_body(q_ref, k_ref, v_ref, r_ref, la_ref, s0_ref, g_ref, None, o_ref, sout_ref,
              S_sc, b_sc, q_sc, **kw)


def _pool_body(*refs, bm, pos0, GC, aliased):
    if aliased:
        p_ref, hist_ref, w_ref, sp_ref, _, o_ref, hout_ref, xa_sc = refs
    else:
        p_ref, hist_ref, w_ref, sp_ref, o_ref, hout_ref, xa_sc = refs
    i = pl.program_id(1)

    @pl.when(i == 0)
    def _():
        xa_sc[0:POOL_CARRY, :] = hist_ref[...]

    xa_sc[POOL_CARRY:POOL_CARRY + bm, :] = p_ref[...]
    pos = pos0 + i * bm + lax.broadcasted_iota(jnp.int32, (bm, 1), 0)
    for gi, w in enumerate(POOL_WINDOWS):
        cs = slice(gi * GC, (gi + 1) * GC)
        x = xa_sc[POOL_CARRY:POOL_CARRY + bm, cs]
        tot = x
        for d in range(1, w):
            tot = tot + xa_sc[POOL_CARRY - d:POOL_CARRY - d + bm, cs]
        cnt = jnp.minimum(pos + 1, w).astype(F32)
        dd = (tot / cnt - x).astype(BF16)
        y = jnp.dot(dd, w_ref[gi], preferred_element_type=F32) * sp_ref[:, cs]
        o_ref[:, cs] = y.astype(o_ref.dtype)
    last = xa_sc[bm:bm + POOL_CARRY, :]
    hout_ref[...] = last
    xa_sc[0:POOL_CARRY, :] = last


def _pool(rest, col, hist, w16, sp, po, *, row0, nseq, nb, bm, pos0):
    M = rest.shape[0]
    G, GC, _ = w16.shape
    PW = G * GC
    r0 = row0 // bm
    assert row0 % bm == 0 and bm >= POOL_CARRY
    in_specs = [pl.BlockSpec((bm, PW), lambda s, i: (r0 + s * nb + i, col)),
                pl.BlockSpec((None, POOL_CARRY, PW), lambda s, i: (s, 0, 0)),
                pl.BlockSpec((G, GC, GC), lambda s, i: (0, 0, 0)),
                pl.BlockSpec((1, PW), lambda s, i: (0, 0))]
    args = [rest, hist, w16, sp.reshape(1, PW)]
    aliases = {}
    if po is not None:
        in_specs.append(pl.BlockSpec(memory_space=pl.ANY))
        args.append(po)
        aliases = {4: 0}
    return pl.pallas_call(
        functools.partial(_pool_body, bm=bm, pos0=pos0, GC=GC, aliased=po is not None),
        grid=(nseq, nb), in_specs=in_specs,
        out_specs=[pl.BlockSpec((bm, PW), lambda s, i: (r0 + s * nb + i, 0)),
                   pl.BlockSpec((None, POOL_CARRY, PW), lambda s, i: (s, 0, 0))],
        out_shape=[jax.ShapeDtypeStruct((M, PW), BF16),
                   jax.ShapeDtypeStruct((nseq, POOL_CARRY, PW), F32)],
        scratch_shapes=[pltpu.VMEM((POOL_CARRY + bm, PW), F32)],
        input_output_aliases=aliases,
        compiler_params=_params("parallel", "arbitrary"), name="pool")(*args)


def kernel(x_prompt, x_sample, cache_fox_k, cache_fox_v, cache_fox_logf, state_gla, state_pool, g_pre_mix, w_in, b_fox_f, w_gla_a2, b_gla_a, g_gla_norm, w_pool, s_pool, w_out, g_post_mix, g_pre_mlp, w_up, w_down, g_post_mlp):
    B, T, D = x_prompt.shape
    SB, ST, _ = x_sample.shape
    L = w_in.shape[0]
    P = cache_fox_k.shape[2]
    assert B == 1
    FW = D // 2
    H = FW // FOX_HD
    GW = D // 4
    DV = GW // GLA_HEADS
    DK = DV // 2
    KW = GLA_HEADS * DK
    RANK = w_gla_a2.shape[1]
    PW = D - FW - GW
    GC = PW // len(POOL_WINDOWS)
    MS = SB * ST
    M = T + MS
    assert H + RANK <= LANES and 2 * KW + 2 * GW == 3 * PW and PW % LANES == 0

    o_fq, o_fk, o_fv, o_ff = 0, FW, 2 * FW, 3 * FW
    o_gq = o_ff + H
    o_ga = o_gq + 2 * KW + GW
    o_gr = o_ga + RANK

    w_out16, w_up16 = (_cast_layer(w, 0) for w in (w_out, w_up))
    w_in16 = w_in.astype(BF16)
    w_rest16 = jnp.concatenate([w_in16[:, :, o_gq:o_ga], w_in16[:, :, o_gr:]], axis=2)
    w_tail16 = jnp.concatenate([w_in16[:, :, o_ff:o_gq], w_in16[:, :, o_ga:o_gr],
                                jnp.zeros((L, D, LANES - H - RANK), BF16)], axis=2)
    w_pool16 = w_pool.astype(BF16)
    past_logf = jnp.pad(cache_fox_logf.astype(F32), ((0, 0), (0, 0), (0, 0), (0, LANES - H)))
    zero_state = jnp.zeros((1, GLA_HEADS, DK, DV), F32)
    zero_hist = jnp.zeros((1, POOL_CARRY, PW), F32)
    sample_hist = jnp.pad(state_pool, ((0, 0), (0, 0), (POOL_CARRY - POOL_HIST, 0), (0, 0)))

    h = None
    x_groups = ((x_prompt.reshape(T, D), 0), (x_sample.reshape(MS, D), T))
    a = None
    for xg, row0 in x_groups:
        a = _norm_first(xg, g_pre_mix[0], a, row0, M)
    outs_p = [[] for _ in range(5)]
    outs_s = [[] for _ in range(5)]
    kv_p, kv_s = [None, None], [None, None]
    for l in range(L):
        bfox_pad = jnp.pad(b_fox_f[l].astype(F32), (0, LANES - H)).reshape(1, LANES)
        w2_pad = jnp.pad(w_gla_a2[l].astype(F32), ((H, LANES - H - RANK), (0, 0)))

        (q16,) = _matmul(a, w_in16, l, [BF16], name="proj_q", col0=o_fq, ncols=FW,
                         scale=FOX_HD ** -0.5 * LOG2E)
        kv16 = []
        for t, (name, col0) in enumerate((("proj_k", o_fk), ("proj_v", o_fv))):
            common = dict(name=name, col0=col0, ncols=FW, layer=l, nlayers=L, hd=FOX_HD)
            kv_p[t], x16 = _matmul_heads(a, w_in16, kv_p[t], None, row0=0, nrows=T, **common)
            kv_s[t], x16 = _matmul_heads(a, w_in16, kv_s[t], x16, row0=T, nrows=MS, **common)
            kv16.append(x16)
        k16, v16 = kv16
        (rest,) = _matmul(a, w_rest16, l, [F32], name="proj_rest")
        (tail,) = _matmul(a, w_tail16, l, [F32], name="proj_tail")
        logf, la = _gates(tail, bfox_pad, w2_pad, b_gla_a[l].reshape(1, KW))

        c_p = _cumsum(logf[:T].reshape(1, T, LANES), 1024)
        bias = _split_bf16(c_p[0, :, :H] * (-LOG2E), BIAS_TERMS)
        kb16 = jnp.pad(bias.reshape(T, H * BIAS_TERMS), ((0, 0), (0, LANES - H * BIAS_TERMS)))
        vt16 = jnp.transpose(v16[:T])
        fo = _fox_prompt(q16, k16, kb16, vt16, T)
        lf_all = jnp.concatenate([past_logf[l], logf[T:].reshape(SB, ST, LANES)], axis=1)
        c_s = _cumsum(lf_all, P + ST)
        ct_s = jnp.transpose(c_s[:, :, :H], (0, 2, 1)).reshape(SB, H, 1, P + ST)
        fo = _fox_sample(fo, q16, k16, v16, cache_fox_k, cache_fox_v, ct_s, l, T, SB, ST)

        go, gla_p = _gla(rest, la, zero_state, g_gla_norm[l], None,
                         row0=0, nseq=1, nc=T // GLA_CHUNK, C=GLA_CHUNK, DK=DK, DV=DV)
        go, gla_s = _gla(rest, la, state_gla[l], g_gla_norm[l], go,
                         row0=T, nseq=SB, nc=1, C=ST, DK=DK, DV=DV)

        pcol = (2 * KW + 2 * GW) // PW
        bp = _pick(T, 512)
        po, hist_p = _pool(rest, pcol, zero_hist, w_pool16[l], s_pool[l], None,
                           row0=0, nseq=1, nb=T // bp, bm=bp, pos0=0)
        po, hist_s = _pool(rest, pcol, sample_hist[l], w_pool16[l], s_pool[l], po,
                           row0=T, nseq=SB, nb=1, bm=ST, pos0=P)

        y = _matmul_cat([fo, go, po], w_out16, 0, name="proj_out")
        if h is None:
            hc = None
            for xg, row0 in x_groups:
                hc = _norm_mid(xg, y, g_post_mix[l], g_pre_mlp[l], hc, row0)
            h, c = hc
        else:
            h, c = _norm_mid(h, y, g_post_mix[l], g_pre_mlp[l])
        later = [(w, l + 1) for w in (w_up, w_out)] if l + 1 < L else []
        hid, w_down16, *nxt = _matmul(c, w_up16, 0, [BF16], name="mlp_up", relu2=True,
                                      side=[(w_down, l)] + later)
        f = _matmul_kacc(hid, w_down16, 0, name="mlp_down")
        if nxt:
            w_up16, w_out16 = nxt
        if l + 1 < L:
            h, a = _norm_mid(h, f, g_post_mlp[l], g_pre_mix[l + 1])
        else:
            y_prompt = _norm_last(h, f, g_post_mlp[l], 0, T)
            y_sample = _norm_last(h, f, g_post_mlp[l], T, MS)

        outs_p[2].append(logf[:T, :H].reshape(1, T, H))
        outs_p[3].append(gla_p)
        outs_p[4].append(hist_p[:, POOL_CARRY - POOL_HIST:])
        outs_s[2].append(logf[T:, :H].reshape(SB, ST, H))
        outs_s[3].append(gla_s)
        outs_s[4].append(hist_s[:, POOL_CARRY - POOL_HIST:])

    p_kv = [x.reshape(L, 1, T, H, FOX_HD) for x in kv_p]
    s_kv = [x.reshape(L, SB, ST, H, FOX_HD) for x in kv_s]
    return (y_prompt.reshape(1, T, D), y_sample.reshape(SB, ST, D),
            *p_kv, *[jnp.stack(o) for o in outs_p[2:]], *s_kv, *[jnp.stack(o) for o in outs_s[2:]])
```
